```python
import math
import jax, jax.numpy as jnp
from jax import lax
import numpy as np

D_MODEL = 4096
BATCH = 8
SEQ = 2048
DEPTH = 2
DEC_BATCH = 16
DEC_SEQ = 16
PAST_LEN = 1024

CHUNK = 64
W_A = D_MODEL
S5_GROUP = 16
N_GROUPS_A = W_A // S5_GROUP
P_STATE = 64
W_B = D_MODEL
N_BLOCKS_B = 16
BLOCK_B = W_B // N_BLOCKS_B
CONV_W = 4
LRU_C = 8.0
D_FF = -(-8 * D_MODEL // (3 * 256)) * 256
IN_COLS = W_A + 2 * W_B + 2 * D_MODEL
EPS = 1e-6

kernel_name = 'hybrid_s5_rglru_stream_step'


def rmsnorm(x, g):
    xf = x.astype(jnp.float32)
    y = xf * lax.rsqrt(jnp.mean(xf * xf, axis=-1, keepdims=True) + EPS)
    return (y * g.astype(jnp.float32)).astype(x.dtype)


def s5_scan(u, h0_re, h0_im, lam_re, lam_im, log_step, b_re, b_im, c_re, c_im, d_skip):
    f32 = jnp.float32
    n, l, _ = u.shape
    blk = min(CHUNK, l)
    nblk = l // blk
    lam = lax.complex(lam_re.astype(f32), lam_im.astype(f32))
    step = jnp.exp(log_step.astype(f32))[:, None]
    a_bar = jnp.exp(lam * step)
    bmat = lax.complex(b_re.astype(f32), b_im.astype(f32))
    b_bar = ((a_bar - 1.0) / lam)[..., None] * bmat
    cmat = lax.complex(c_re.astype(f32), c_im.astype(f32))
    t_idx = jnp.arange(1, blk + 1, dtype=f32)
    a_pow = jnp.exp(lam[None] * (step[None] * t_idx[:, None, None]))
    ub_all = u.astype(f32).reshape(n, nblk, blk, N_GROUPS_A, S5_GROUP).transpose(1, 0, 2, 3, 4)

    def combine(e1, e2):
        a1, b1 = e1
        a2, b2 = e2
        return a1 * a2, a2 * b1 + b2

    def block_step(h, ub):
        bu = jnp.einsum('nlgh,gph->nlgp', ub.astype(jnp.complex64), b_bar)
        a = jnp.broadcast_to(a_bar, bu.shape)
        _, hs = lax.associative_scan(combine, (a, bu), axis=1)
        hs = hs + a_pow[None] * h[:, None]
        y = jnp.einsum('nlgp,ghp->nlgh', hs, cmat).real
        return hs[:, -1], y

    h0 = lax.complex(h0_re.astype(f32), h0_im.astype(f32))
    h_last, ys = lax.scan(block_step, h0, ub_all)
    y = ys.transpose(1, 0, 2, 3, 4).reshape(n, l, W_A) + d_skip.astype(f32) * u.astype(f32)
    return y, h_last.real, h_last.imag


def rglru_branch(xb, conv_buf, h0, conv_w, conv_b, w_r, b_r, w_i, b_i, lru_lam):
    f32 = jnp.float32
    n, l, _ = xb.shape
    xp = jnp.concatenate([conv_buf.astype(f32), xb.astype(f32)], axis=1)
    cw = conv_w.astype(f32)
    xc = conv_b.astype(f32) + sum(cw[k] * xp[:, k:k + l] for k in range(CONV_W))
    new_buf = xp[:, -(CONV_W - 1):]
    xr = xc.reshape(n, l, N_BLOCKS_B, BLOCK_B)
    r = jax.nn.sigmoid(jnp.einsum('nlkc,kcd->nlkd', xr, w_r.astype(f32)) + b_r.astype(f32).reshape(N_BLOCKS_B, BLOCK_B))
    i = jax.nn.sigmoid(jnp.einsum('nlkc,kcd->nlkd', xr, w_i.astype(f32)) + b_i.astype(f32).reshape(N_BLOCKS_B, BLOCK_B))
    r = r.reshape(n, l, W_B)
    i = i.reshape(n, l, W_B)
    log_a = -LRU_C * r * jax.nn.softplus(-lru_lam.astype(f32))
    a = jnp.exp(log_a)
    bx = jnp.sqrt(-jnp.expm1(2.0 * log_a)) * (i * xc)

    def step(h, ab):
        a_t, b_t = ab
        h = a_t * h + b_t
        return h, h

    h_last, hs = lax.scan(step, h0.astype(f32), (a.swapaxes(0, 1), bx.swapaxes(0, 1)))
    return hs.swapaxes(0, 1), h_last, new_buf


def layer(x, s5_re, s5_im, lru_h, conv_buf, g_pre_mix, w_in, lam_re, lam_im, log_step, b_re, b_im,
          c_re, c_im, d_skip, w_glu, b_glu, conv_w, conv_b, w_r, b_r, w_i, b_i, lru_lam, p_a, p_b,
          w_out, g_post_mix, g_pre_ffn, w_gate, w_up, w_down, g_post_ffn):
    dt = x.dtype
    h = rmsnorm(x, g_pre_mix)
    z = h @ w_in
    u_a, x_b, gate_b, g_a, g_b = jnp.split(
        z, [W_A, W_A + W_B, W_A + 2 * W_B, W_A + 2 * W_B + D_MODEL], axis=-1)
    y_a, s5_re_new, s5_im_new = s5_scan(u_a, s5_re, s5_im, lam_re, lam_im, log_step,
                                        b_re, b_im, c_re, c_im, d_skip)
    y_a = jax.nn.gelu(y_a).astype(dt)
    y_a = y_a * jax.nn.sigmoid(y_a @ w_glu + b_glu)
    y_b, lru_new, conv_new = rglru_branch(x_b, conv_buf, lru_h, conv_w, conv_b, w_r, b_r, w_i, b_i, lru_lam)
    y_b = y_b.astype(dt) * jax.nn.gelu(gate_b)
    merged = jax.nn.sigmoid(g_a) * (y_a @ p_a) + jax.nn.sigmoid(g_b) * (y_b @ p_b)
    x = x + rmsnorm(merged @ w_out, g_post_mix)
    h = rmsnorm(x, g_pre_ffn)
    f = (jax.nn.silu(h @ w_gate) * (h @ w_up)) @ w_down
    x = x + rmsnorm(f, g_post_ffn)
    return x, s5_re_new, s5_im_new, lru_new, conv_new


def setup_inputs(seed: int = 0) -> dict:
    key = jax.random.key(seed)
    ks = iter(jax.random.split(key, 40))
    f32 = jnp.float32

    def nrm(shape, scale):
        return jax.random.normal(next(ks), shape, f32) * scale

    def gain():
        return 1.0 + nrm((DEPTH, D_MODEL), 0.02)

    x_prompt = nrm((BATCH, SEQ, D_MODEL), 1.0)
    x_sample = nrm((DEC_BATCH, DEC_SEQ, D_MODEL), 1.0)
    state_s5_re = nrm((DEPTH, DEC_BATCH, N_GROUPS_A, P_STATE), 0.5)
    state_s5_im = nrm((DEPTH, DEC_BATCH, N_GROUPS_A, P_STATE), 0.5)
    state_lru = nrm((DEPTH, DEC_BATCH, W_B), 0.5)
    cache_conv = nrm((DEPTH, DEC_BATCH, CONV_W - 1, W_B), 1.0)

    g_pre_mix = gain()
    w_in = nrm((DEPTH, D_MODEL, IN_COLS), D_MODEL ** -0.5)
    s5_lam_re = -0.5 + nrm((DEPTH, N_GROUPS_A, P_STATE), 0.01)
    s5_lam_im = math.pi * jnp.arange(P_STATE, dtype=f32) + nrm((DEPTH, N_GROUPS_A, P_STATE), 0.01)
    s5_log_step = jax.random.uniform(next(ks), (DEPTH, N_GROUPS_A), f32, math.log(1e-3), math.log(1e-1))
    s5_b_re = nrm((DEPTH, N_GROUPS_A, P_STATE, S5_GROUP), (2 * S5_GROUP) ** -0.5)
    s5_b_im = nrm((DEPTH, N_GROUPS_A, P_STATE, S5_GROUP), (2 * S5_GROUP) ** -0.5)
    s5_c_re = nrm((DEPTH, N_GROUPS_A, S5_GROUP, P_STATE), (2 * P_STATE) ** -0.5)
    s5_c_im = nrm((DEPTH, N_GROUPS_A, S5_GROUP, P_STATE), (2 * P_STATE) ** -0.5)
    s5_d = nrm((DEPTH, W_A), 1.0)
    w_glu = nrm((DEPTH, W_A, W_A), W_A ** -0.5)
    b_glu = nrm((DEPTH, W_A), 0.01)
    conv_w = nrm((DEPTH, CONV_W, W_B), CONV_W ** -0.5)
    conv_b = nrm((DEPTH, W_B), 0.01)
    lru_w_r = nrm((DEPTH, N_BLOCKS_B, BLOCK_B, BLOCK_B), BLOCK_B ** -0.5)
    lru_b_r = nrm((DEPTH, W_B), 0.01)
    lru_w_i = nrm((DEPTH, N_BLOCKS_B, BLOCK_B, BLOCK_B), BLOCK_B ** -0.5)
    lru_b_i = nrm((DEPTH, W_B), 0.01)
    a_target = jax.random.uniform(next(ks), (DEPTH, W_B), f32, 0.9, 0.999)
    v = a_target ** (1.0 / LRU_C)
    lru_lam = jnp.log(v) - jnp.log1p(-v)
    p_a = nrm((DEPTH, W_A, D_MODEL), W_A ** -0.5)
    p_b = nrm((DEPTH, W_B, D_MODEL), W_B ** -0.5)
    w_out = nrm((DEPTH, D_MODEL, D_MODEL), D_MODEL ** -0.5)
    g_post_mix = gain()
    g_pre_ffn = gain()
    w_gate = nrm((DEPTH, D_MODEL, D_FF), D_MODEL ** -0.5)
    w_up = nrm((DEPTH, D_MODEL, D_FF), D_MODEL ** -0.5)
    w_down = nrm((DEPTH, D_FF, D_MODEL), D_FF ** -0.5)
    g_post_ffn = gain()
    return {'x_prompt': x_prompt, 'x_sample': x_sample, 'state_s5_re': state_s5_re,
            'state_s5_im': state_s5_im, 'state_lru': state_lru, 'cache_conv': cache_conv,
            'g_pre_mix': g_pre_mix, 'w_in': w_in, 's5_lam_re': s5_lam_re, 's5_lam_im': s5_lam_im,
            's5_log_step': s5_log_step, 's5_b_re': s5_b_re, 's5_b_im': s5_b_im, 's5_c_re': s5_c_re,
            's5_c_im': s5_c_im, 's5_d': s5_d, 'w_glu': w_glu, 'b_glu': b_glu, 'conv_w': conv_w,
            'conv_b': conv_b, 'lru_w_r': lru_w_r, 'lru_b_r': lru_b_r, 'lru_w_i': lru_w_i,
            'lru_b_i': lru_b_i, 'lru_lam': lru_lam, 'p_a': p_a, 'p_b': p_b, 'w_out': w_out,
            'g_post_mix': g_post_mix, 'g_pre_ffn': g_pre_ffn, 'w_gate': w_gate, 'w_up': w_up,
            'w_down': w_down, 'g_post_ffn': g_post_ffn}


def reference(x_prompt, x_sample, state_s5_re, state_s5_im, state_lru, cache_conv,
              g_pre_mix, w_in, s5_lam_re, s5_lam_im, s5_log_step, s5_b_re, s5_b_im, s5_c_re,
              s5_c_im, s5_d, w_glu, b_glu, conv_w, conv_b, lru_w_r, lru_b_r, lru_w_i, lru_b_i,
              lru_lam, p_a, p_b, w_out, g_post_mix, g_pre_ffn, w_gate, w_up, w_down, g_post_ffn):
    nb = x_prompt.shape[0]
    f32 = jnp.float32
    xp, xs = x_prompt, x_sample
    p_re, p_im, p_lru, p_conv = [], [], [], []
    s_re, s_im, s_lru, s_conv = [], [], [], []
    for l in range(DEPTH):
        lp = (g_pre_mix[l], w_in[l], s5_lam_re[l], s5_lam_im[l], s5_log_step[l], s5_b_re[l],
              s5_b_im[l], s5_c_re[l], s5_c_im[l], s5_d[l], w_glu[l], b_glu[l], conv_w[l], conv_b[l],
              lru_w_r[l], lru_b_r[l], lru_w_i[l], lru_b_i[l], lru_lam[l], p_a[l], p_b[l], w_out[l],
              g_post_mix[l], g_pre_ffn[l], w_gate[l], w_up[l], w_down[l], g_post_ffn[l])
        xp, a_re, a_im, a_lru, a_conv = layer(
            xp, jnp.zeros((nb, N_GROUPS_A, P_STATE), f32), jnp.zeros((nb, N_GROUPS_A, P_STATE), f32),
            jnp.zeros((nb, W_B), f32), jnp.zeros((nb, CONV_W - 1, W_B), f32), *lp)
        xs, b_re_, b_im_, b_lru, b_conv = layer(
            xs, state_s5_re[l], state_s5_im[l], state_lru[l], cache_conv[l], *lp)
        p_re.append(a_re); p_im.append(a_im); p_lru.append(a_lru); p_conv.append(a_conv)
        s_re.append(b_re_); s_im.append(b_im_); s_lru.append(b_lru); s_conv.append(b_conv)
    sd = state_lru.dtype
    new_s5_re_prompt = jnp.stack(p_re).astype(sd)
    new_s5_im_prompt = jnp.stack(p_im).astype(sd)
    new_lru_prompt = jnp.stack(p_lru).astype(sd)
    new_conv_prompt = jnp.stack(p_conv).astype(sd)
    new_s5_re_sample = jnp.stack(s_re).astype(sd)
    new_s5_im_sample = jnp.stack(s_im).astype(sd)
    new_lru_sample = jnp.stack(s_lru).astype(sd)
    new_conv_sample = jnp.stack(s_conv).astype(sd)
    return (xp, xs, new_s5_re_prompt, new_s5_im_prompt, new_lru_prompt, new_conv_prompt,
            new_s5_re_sample, new_s5_im_sample, new_lru_sample, new_conv_sample)
```

```python
import functools
import math

import jax
import jax.numpy as jnp
from jax import lax
from jax.experimental import pallas as pl
from jax.experimental.pallas import tpu as pltpu

F32 = jnp.float32
BF16 = jnp.bfloat16

D_MODEL = 4096
S5_GROUP = 16
N_GROUPS = D_MODEL // S5_GROUP
P_STATE = 64
S5_CHUNK = 16
N_BLOCKS_B = 16
BLOCK_B = D_MODEL // N_BLOCKS_B
CONV_W = 4
LRU_C = 8.0
D_FF = 11008
EPS = 1e-6
LANES = 128
MIB = 1024 * 1024


def _cparams(sem, vmem_mib):
    return pltpu.CompilerParams(dimension_semantics=sem, vmem_limit_bytes=vmem_mib * MIB)


def _sigmoid(x):
    return 0.5 * (jnp.tanh(0.5 * x) + 1.0)


def _gelu(x):
    c = math.sqrt(2.0 / math.pi)
    return 0.5 * x * (1.0 + jnp.tanh(c * (x + 0.044715 * (x * x * x))))


def _dot(a, b):
    return jnp.dot(a, b, preferred_element_type=F32)


def _dot_nt(a, b, precision=None):
    return lax.dot_general(a, b, (((1,), (1,)), ((), ())), precision=precision,
                           preferred_element_type=F32)


def _s5_prep_kernel(lre_ref, lim_ref, ls_ref, bre_ref, bim_ref, cre_ref, cim_ref,
                    kb_ref, ct_ref, p_ref, q_ref, *, gb):
    tau = lax.broadcasted_iota(jnp.int32, (24, P_STATE), 0).astype(F32)
    lane = lax.broadcasted_iota(jnp.int32, (S5_GROUP, 2 * LANES), 1)
    for g in range(gb):
        lr = lre_ref[0, g:g + 1, :]
        li = lim_ref[0, g:g + 1, :]
        step = jnp.exp(ls_ref[0, g:g + 1, :])
        xr = lr * step
        xi = li * step
        mag = jnp.exp(xr * tau)
        ang = xi * tau
        pw_r = mag * jnp.cos(ang)
        pw_i = mag * jnp.sin(ang)
        a_r = pw_r[1:2]
        a_i = pw_i[1:2]
        den = lr * lr + li * li
        n_r = a_r - 1.0
        n_i = a_i
        co_r = (n_r * lr + n_i * li) / den
        co_i = (n_i * lr - n_r * li) / den
        bt_r = bre_ref[0, g]
        bt_i = bim_ref[0, g]
        bb_r = co_r * bt_r - co_i * bt_i
        bb_i = co_r * bt_i + co_i * bt_r
        c_r = cre_ref[0, g]
        c_i = cim_ref[0, g]
        ca = []
        for t in range(S5_CHUNK + 1):
            e_r = pw_r[t:t + 1]
            e_i = pw_i[t:t + 1]
            re = c_r * e_r - c_i * e_i
            im = c_r * e_i + c_i * e_r
            ca.append(jnp.concatenate([re, -im], axis=1))
        for t in range(S5_CHUNK):
            ct_ref[0, g, t * 16:(t + 1) * 16, :] = ca[t + 1].astype(BF16)
        ca_all = jnp.concatenate(ca[:S5_CHUNK], axis=0)
        b2 = jnp.concatenate([bb_r, bb_i], axis=1)
        strip = _dot_nt(b2, ca_all, precision=lax.Precision.HIGHEST)
        for s in range(S5_CHUNK):
            if s == 0:
                blk = strip
            else:
                blk = jnp.where(lane >= 16 * s, pltpu.roll(strip, 16 * s, 1), 0.0)
            kb_ref[0, g, s * 16:(s + 1) * 16, 0:256] = blk.astype(BF16)
            e_r = pw_r[15 - s:16 - s]
            e_i = pw_i[15 - s:16 - s]
            bm_r = bb_r * e_r - bb_i * e_i
            bm_i = bb_r * e_i + bb_i * e_r
            kb_ref[0, g, s * 16:(s + 1) * 16, 256:384] = jnp.concatenate(
                [bm_r, bm_i], axis=1).astype(BF16)
        a16_r = pw_r[16:17]
        a16_i = pw_i[16:17]
        p_ref[0, g:g + 1, :] = jnp.concatenate([a16_r, a16_r], axis=1)
        q_ref[0, g:g + 1, :] = jnp.concatenate([-a16_i, a16_i], axis=1)


def _s5_prep(lam_re, lam_im, log_step, bt_re, bt_im, c_re, c_im):
    depth = lam_re.shape[0]
    gb = 8
    grid = (depth, N_GROUPS // gb)
    m3 = lambda l, g: (l, g, 0)
    m4 = lambda l, g: (l, g, 0, 0)
    return pl.pallas_call(
        functools.partial(_s5_prep_kernel, gb=gb),
        grid=grid,
        in_specs=[
            pl.BlockSpec((1, gb, P_STATE), m3),
            pl.BlockSpec((1, gb, P_STATE), m3),
            pl.BlockSpec((1, gb, 1), m3),
            pl.BlockSpec((1, gb, S5_GROUP, P_STATE), m4),
            pl.BlockSpec((1, gb, S5_GROUP, P_STATE), m4),
            pl.BlockSpec((1, gb, S5_GROUP, P_STATE), m4),
            pl.BlockSpec((1, gb, S5_GROUP, P_STATE), m4),
        ],
        out_specs=[
            pl.BlockSpec((1, gb, 256, 384), m4),
            pl.BlockSpec((1, gb, 256, 128), m4),
            pl.BlockSpec((1, gb, 128), m3),
            pl.BlockSpec((1, gb, 128), m3),
        ],
        out_shape=[
            jax.ShapeDtypeStruct((depth, N_GROUPS, 256, 384), BF16),
            jax.ShapeDtypeStruct((depth, N_GROUPS, 256, 128), BF16),
            jax.ShapeDtypeStruct((depth, N_GROUPS, 128), F32),
            jax.ShapeDtypeStruct((depth, N_GROUPS, 128), F32),
        ],
        compiler_params=_cparams(("arbitrary", "arbitrary"), 32),
        name="s5_prep",
    )(lam_re, lam_im, log_step.reshape(depth, N_GROUPS, 1), bt_re, bt_im, c_re, c_im)


def _block_transpose8(arrs, masks):
    arrs = list(arrs)
    for k in range(3):
        d = 1 << k
        sh = 16 * d
        new = list(arrs)
        for a in range(8):
            if a & d:
                continue
            b = a | d
            va, vb = arrs[a], arrs[b]
            new[a] = jnp.where(masks[k], pltpu.roll(vb, sh, 1), va)
            new[b] = jnp.where(masks[k], vb, pltpu.roll(va, LANES - sh, 1))
        arrs = new
    return arrs


def _s5_kernel(u_ref, kb_ref, ct_ref, p_ref, q_ref, d_ref, h0_ref,
               yf_ref, yb_ref, hout_ref,
               uflat, hb, hprev, ybuf, carry, *, nb, tl, gps):
    i = pl.program_id(1)
    nt = pl.num_programs(1)
    nc = tl // S5_CHUNK
    cb = min(8, nc)
    nsub = nc // cb
    sr = cb * nb
    nh = gps // 8

    @pl.when(i == 0)
    def _():
        carry[...] = h0_ref[...]

    lane_blk = lax.broadcasted_iota(jnp.int32, (sr, LANES), 1) // 16
    masks = [((lane_blk >> k) & 1) == 1 for k in range(3)]

    def fwd(j, _):
        base = pl.multiple_of(j * (cb * S5_CHUNK * nb), 8)
        r0 = pl.multiple_of(j * sr, 16)
        for hh in range(nh):
            for sh in range(2):
                arrs = []
                for sl in range(8):
                    s = sh * 8 + sl
                    pieces = [u_ref[pl.ds(base + (cc * S5_CHUNK + s) * nb, nb),
                                    hh * LANES:(hh + 1) * LANES] for cc in range(cb)]
                    arrs.append(pieces[0] if cb == 1 else jnp.concatenate(pieces, axis=0))
                w = _block_transpose8(arrs, masks)
                for g in range(8):
                    uflat[hh * 8 + g, pl.ds(r0, sr), sh * LANES:(sh + 1) * LANES] = (
                        w[g].astype(BF16))
        return 0

    lax.fori_loop(0, nsub, fwd, 0)

    for gi in range(gps):
        hb[gi] = _dot(uflat[gi], kb_ref[gi, :, 256:384])

    def step(c, hs):
        rows = pl.ds(pl.multiple_of(c * nb, 8), nb)
        out = []
        for gi in range(gps):
            h = hs[gi]
            hprev[gi, rows, :] = h
            pg = p_ref[gi:gi + 1, :]
            qg = q_ref[gi:gi + 1, :]
            out.append(pg * h + qg * pltpu.roll(h, P_STATE, 1) + hb[gi, rows, :])
        return tuple(out)

    hs = lax.fori_loop(0, nc, step, tuple(carry[gi] for gi in range(gps)))
    for gi in range(gps):
        carry[gi] = hs[gi]

    for gi in range(gps):
        ybuf[gi] = (_dot(uflat[gi], kb_ref[gi, :, 0:256])
                    + _dot_nt(hprev[gi].astype(BF16), ct_ref[gi]))

    def inv(j, _):
        base = pl.multiple_of(j * (cb * S5_CHUNK * nb), 8)
        r0 = pl.multiple_of(j * sr, 16)
        for hh in range(nh):
            lanes = slice(hh * LANES, (hh + 1) * LANES)
            dsk = d_ref[:, lanes]
            for sh in range(2):
                arrs = [ybuf[hh * 8 + g, pl.ds(r0, sr), sh * LANES:(sh + 1) * LANES]
                        for g in range(8)]
                z = _block_transpose8(arrs, masks)
                for sl in range(0, 8, 2):
                    t = sh * 8 + sl
                    for cc in range(cb):
                        rows = pl.ds(base + (cc * S5_CHUNK + t) * nb, 2 * nb)
                        y2 = jnp.concatenate([z[sl][cc * nb:(cc + 1) * nb, :],
                                              z[sl + 1][cc * nb:(cc + 1) * nb, :]], axis=0)
                        ya = _gelu(y2 + dsk * u_ref[rows, lanes])
                        yf_ref[rows, lanes] = ya
                        yb_ref[rows, lanes] = ya.astype(BF16)
        return 0

    lax.fori_loop(0, nsub, inv, 0)

    @pl.when(i == nt - 1)
    def _():
        hout_ref[...] = carry[...]


def _s5_branch(z, kb, ct, pp, qq, dskip, h0, *, nb, seq, tl):
    gps = 16
    tm = tl * nb
    nt = seq // tl
    nblk = N_GROUPS // gps
    cw = gps * S5_GROUP
    nc = tl // S5_CHUNK
    r = nc * nb
    t_rows = seq * nb
    return pl.pallas_call(
        functools.partial(_s5_kernel, nb=nb, tl=tl, gps=gps),
        grid=(nblk, nt),
        in_specs=[
            pl.BlockSpec((tm, cw), lambda b, i: (i, b)),
            pl.BlockSpec((gps, 256, 384), lambda b, i: (b, 0, 0)),
            pl.BlockSpec((gps, 256, 128), lambda b, i: (b, 0, 0)),
            pl.BlockSpec((gps, 128), lambda b, i: (b, 0)),
            pl.BlockSpec((gps, 128), lambda b, i: (b, 0)),
            pl.BlockSpec((1, cw), lambda b, i: (0, b)),
            pl.BlockSpec((gps, nb, 128), lambda b, i: (b, 0, 0)),
        ],
        out_specs=[
            pl.BlockSpec((tm, cw), lambda b, i: (i, b)),
            pl.BlockSpec((tm, cw), lambda b, i: (i, b)),
            pl.BlockSpec((gps, nb, 128), lambda b, i: (b, 0, 0)),
        ],
        out_shape=[
            jax.ShapeDtypeStruct((t_rows, D_MODEL), F32),
            jax.ShapeDtypeStruct((t_rows, D_MODEL), BF16),
            jax.ShapeDtypeStruct((N_GROUPS, nb, 128), F32),
        ],
        scratch_shapes=[
            pltpu.VMEM((gps, r, 256), BF16),
            pltpu.VMEM((gps, r, 128), F32),
            pltpu.VMEM((gps, r, 128), F32),
            pltpu.VMEM((gps, r, 256), F32),
            pltpu.VMEM((gps, nb, 128), F32),
        ],
        compiler_params=_cparams(("arbitrary", "arbitrary"), 56),
        name="s5_branch",
    )(z, kb, ct, pp, qq, dskip, h0)


def _lru_kernel(xb_ref, gt_ref, cw_ref, cb_ref, wri_ref, br_ref, bi_ref, lam_ref,
                h0_ref, cbuf_ref,
                y_ref, hout_ref, cout_ref,
                xs, a_s, b_s, hc, *, nb, tl, rc):
    i = pl.program_id(1)
    nt = pl.num_programs(1)
    tm = tl * nb
    hist = (CONV_W - 1) * nb

    @pl.when(i == 0)
    def _():
        hc[...] = h0_ref[...]
        xs[0:hist, :] = cbuf_ref[...]

    @pl.when(i > 0)
    def _():
        xs[0:hist, :] = xs[tm:tm + hist, :]

    xs[hist:hist + tm, :] = xb_ref[...]

    cw = cw_ref[...]
    cbias = cb_ref[...]
    br = br_ref[...]
    bi = bi_ref[...]
    nlam = -lam_ref[...]
    sp = jnp.maximum(nlam, 0.0) + jnp.log1p(jnp.exp(-jnp.abs(nlam)))

    def gates(c, _):
        r0 = pl.multiple_of(c * rc, rc)
        acc = cw[0:1] * xs[pl.ds(r0, rc), :]
        for k in range(1, CONV_W):
            acc = acc + cw[k:k + 1] * xs[pl.ds(r0 + k * nb, rc), :]
        xc = cbias + acc
        gm = _dot(xc.astype(BF16), wri_ref[0])
        r = _sigmoid(gm[:, 0:BLOCK_B] + br)
        ig = _sigmoid(gm[:, BLOCK_B:2 * BLOCK_B] + bi)
        log_a = -LRU_C * r * sp
        a_s[pl.ds(r0, rc), :] = jnp.exp(log_a)
        b_s[pl.ds(r0, rc), :] = jnp.sqrt(1.0 - jnp.exp(2.0 * log_a)) * (ig * xc)
        return 0

    lax.fori_loop(0, tm // rc, gates, 0)

    def scan(t, h):
        rows = pl.ds(pl.multiple_of(t * nb, 8), nb)
        h = a_s[rows, :] * h + b_s[rows, :]
        b_s[rows, :] = h
        return h

    h = lax.fori_loop(0, tl, scan, hc[...], unroll=8)
    hc[...] = h

    def outp(c, _):
        rows = pl.ds(pl.multiple_of(c * rc, rc), rc)
        y_ref[rows, :] = (b_s[rows, :] * _gelu(gt_ref[rows, :])).astype(BF16)
        return 0

    lax.fori_loop(0, tm // rc, outp, 0)

    @pl.when(i == nt - 1)
    def _():
        hout_ref[...] = h
        cout_ref[...] = xs[tm:tm + hist, :]


def _lru_branch(z, conv_w, conv_b, wri, b_r, b_i, lam, h0, cbuf, *, nb, seq, tl):
    tm = tl * nb
    nt = seq // tl
    rc = min(256, tm)
    hist = (CONV_W - 1) * nb
    xoff = D_MODEL // BLOCK_B
    col = lambda k, i: (0, k)
    return pl.pallas_call(
        functools.partial(_lru_kernel, nb=nb, tl=tl, rc=rc),
        grid=(N_BLOCKS_B, nt),
        in_specs=[
            pl.BlockSpec((tm, BLOCK_B), lambda k, i: (i, xoff + k)),
            pl.BlockSpec((tm, BLOCK_B), lambda k, i: (i, 2 * xoff + k)),
            pl.BlockSpec((CONV_W, BLOCK_B), col),
            pl.BlockSpec((1, BLOCK_B), col),
            pl.BlockSpec((1, BLOCK_B, 2 * BLOCK_B), lambda k, i: (k, 0, 0)),
            pl.BlockSpec((1, BLOCK_B), col),
            pl.BlockSpec((1, BLOCK_B), col),
            pl.BlockSpec((1, BLOCK_B), col),
            pl.BlockSpec((nb, BLOCK_B), col),
            pl.BlockSpec((hist, BLOCK_B), col),
        ],
        out_specs=[
            pl.BlockSpec((tm, BLOCK_B), lambda k, i: (i, k)),
            pl.BlockSpec((nb, BLOCK_B), col),
            pl.BlockSpec((hist, BLOCK_B), col),
        ],
        out_shape=[
            jax.ShapeDtypeStruct((seq * nb, D_MODEL), BF16),
            jax.ShapeDtypeStruct((nb, D_MODEL), F32),
            jax.ShapeDtypeStruct((hist, D_MODEL), F32),
        ],
        scratch_shapes=[
            pltpu.VMEM((tm + hist, BLOCK_B), F32),
            pltpu.VMEM((tm, BLOCK_B), F32),
            pltpu.VMEM((tm, BLOCK_B), F32),
            pltpu.VMEM((nb, BLOCK_B), F32),
        ],
        compiler_params=_cparams(("arbitrary", "arbitrary"), 48),
        name="lru_branch",
    )(z, z, conv_w, conv_b, wri, b_r, b_i, lam, h0, cbuf)


def _rmsnorm_kernel(x_ref, g_ref, o_ref):
    x = x_ref[...]
    inv = lax.rsqrt(jnp.mean(x * x, axis=-1, keepdims=True) + EPS)
    o_ref[...] = (x * inv * g_ref[...]).astype(o_ref.dtype)


def _rmsnorm(x, g):
    t = x.shape[0]
    tr = min(512, t)
    return pl.pallas_call(
        _rmsnorm_kernel,
        grid=(t // tr,),
        in_specs=[pl.BlockSpec((tr, D_MODEL), lambda i: (i, 0)),
                  pl.BlockSpec((1, D_MODEL), lambda i: (0, 0))],
        out_specs=pl.BlockSpec((tr, D_MODEL), lambda i: (i, 0)),
        out_shape=jax.ShapeDtypeStruct((t, D_MODEL), BF16),
        compiler_params=_cparams(("arbitrary",), 40),
        name="rmsnorm",
    )(x, g)


def _mm_kernel(a_ref, w_ref, o_ref):
    o_ref[...] = _dot(a_ref[...], w_ref[...]).astype(o_ref.dtype)


def _matmul(a, w, *, tm, tn, out_dtype):
    t, k = a.shape
    n = w.shape[1]
    return pl.pallas_call(
        _mm_kernel,
        grid=(t // tm, n // tn),
        in_specs=[pl.BlockSpec((tm, k), lambda i, j: (i, 0)),
                  pl.BlockSpec((k, tn), lambda i, j: (0, j))],
        out_specs=pl.BlockSpec((tm, tn), lambda i, j: (i, j)),
        out_shape=jax.ShapeDtypeStruct((t, n), out_dtype),
        compiler_params=_cparams(("arbitrary", "arbitrary"), 48),
        name="matmul",
    )(a, w)


def _glu_kernel(a_ref, w_ref, b_ref, y_ref, o_ref):
    gate = _sigmoid(_dot(a_ref[...], w_ref[...]) + b_ref[...])
    o_ref[...] = (y_ref[...] * gate).astype(o_ref.dtype)


def _glu(ya_b, ya_f, w, b, *, tm, tn):
    t = ya_b.shape[0]
    return pl.pallas_call(
        _glu_kernel,
        grid=(t // tm, D_MODEL // tn),
        in_specs=[pl.BlockSpec((tm, D_MODEL), lambda i, j: (i, 0)),
                  pl.BlockSpec((D_MODEL, tn), lambda i, j: (0, j)),
                  pl.BlockSpec((1, tn), lambda i, j: (0, j)),
                  pl.BlockSpec((tm, tn), lambda i, j: (i, j))],
        out_specs=pl.BlockSpec((tm, tn), lambda i, j: (i, j)),
        out_shape=jax.ShapeDtypeStruct((t, D_MODEL), BF16),
        compiler_params=_cparams(("arbitrary", "arbitrary"), 48),
        name="glu",
    )(ya_b, w, b, ya_f)


def _merge_kernel(a1_ref, w1_ref, a2_ref, w2_ref, ga_ref, gb_ref, o_ref):
    m = (_sigmoid(ga_ref[...]) * _dot(a1_ref[...], w1_ref[...])
         + _sigmoid(gb_ref[...]) * _dot(a2_ref[...], w2_ref[...]))
    o_ref[...] = m.astype(o_ref.dtype)


def _merge(ya, p_a, yb, p_b, z, *, tm, tn):
    t = ya.shape[0]
    ga_off = 3 * D_MODEL // tn
    gb_off = 4 * D_MODEL // tn
    return pl.pallas_call(
        _merge_kernel,
        grid=(t // tm, D_MODEL // tn),
        in_specs=[pl.BlockSpec((tm, D_MODEL), lambda i, j: (i, 0)),
                  pl.BlockSpec((D_MODEL, tn), lambda i, j: (0, j)),
                  pl.BlockSpec((tm, D_MODEL), lambda i, j: (i, 0)),
                  pl.BlockSpec((D_MODEL, tn), lambda i, j: (0, j)),
                  pl.BlockSpec((tm, tn), lambda i, j: (i, ga_off + j)),
                  pl.BlockSpec((tm, tn), lambda i, j: (i, gb_off + j))],
        out_specs=pl.BlockSpec((tm, tn), lambda i, j: (i, j)),
        out_shape=jax.ShapeDtypeStruct((t, D_MODEL), BF16),
        compiler_params=_cparams(("arbitrary", "arbitrary"), 56),
        name="merge",
    )(ya, p_a, yb, p_b, z, z)


def _outnorm_kernel(a_ref, w_ref, x_ref, g_ref, o_ref, fbuf, ssq, *, nj):
    j = pl.program_id(1)

    @pl.when(j == 0)
    def _():
        ssq[...] = jnp.zeros_like(ssq)

    @pl.when(j < nj)
    def _():
        f = _dot(a_ref[...], w_ref[...])
        fbuf[j] = f
        ssq[...] += jnp.sum(f * f, axis=1, keepdims=True)

    @pl.when(j >= nj)
    def _():
        inv = lax.rsqrt(ssq[...] * (1.0 / D_MODEL) + EPS)
        o_ref[...] = x_ref[...] + fbuf[j - nj] * inv * g_ref[...]


def _outnorm(a, w, x, g, *, tm, tn):
    t, k = a.shape
    nj = D_MODEL // tn
    mm_col = lambda i, j: (0, jnp.minimum(j, nj - 1))
    out_col = lambda i, j: (i, jnp.maximum(j - nj, 0))
    return pl.pallas_call(
        functools.partial(_outnorm_kernel, nj=nj),
        grid=(t // tm, 2 * nj),
        in_specs=[pl.BlockSpec((tm, k), lambda i, j: (i, 0)),
                  pl.BlockSpec((k, tn), mm_col),
                  pl.BlockSpec((tm, tn), out_col),
                  pl.BlockSpec((1, tn), lambda i, j: (0, jnp.maximum(j - nj, 0)))],
        out_specs=pl.BlockSpec((tm, tn), out_col),
        out_shape=jax.ShapeDtypeStruct((t, D_MODEL), F32),
        scratch_shapes=[pltpu.VMEM((nj, tm, tn), F32),
                        pltpu.VMEM((tm, 1), F32)],
        compiler_params=_cparams(("arbitrary", "arbitrary"), 56),
        name="outnorm",
    )(a, w, x, g)


def _ffn_up_kernel(a_ref, wg_ref, wu_ref, o_ref):
    a = a_ref[...]
    g = _dot(a, wg_ref[...])
    u = _dot(a, wu_ref[...])
    o_ref[...] = (g * _sigmoid(g) * u).astype(o_ref.dtype)


def _ffn_up(h, wg, wu, *, tm, tn):
    t = h.shape[0]
    return pl.pallas_call(
        _ffn_up_kernel,
        grid=(t // tm, D_FF // tn),
        in_specs=[pl.BlockSpec((tm, D_MODEL), lambda i, j: (i, 0)),
                  pl.BlockSpec((D_MODEL, tn), lambda i, j: (0, j)),
                  pl.BlockSpec((D_MODEL, tn), lambda i, j: (0, j))],
        out_specs=pl.BlockSpec((tm, tn), lambda i, j: (i, j)),
        out_shape=jax.ShapeDtypeStruct((t, D_FF), BF16),
        compiler_params=_cparams(("arbitrary", "arbitrary"), 48),
        name="ffn_up",
    )(h, wg, wu)


def _layer(x, s5_h0, lru_h0, conv_buf, w, *, nb, seq, tl):
    tm = min(1024, seq * nb)
    h = _rmsnorm(x, w["g_pre_mix"])
    z = _matmul(h, w["w_in"], tm=tm, tn=512, out_dtype=F32)
    ya_f, ya_b, s5_new = _s5_branch(z, w["kb"], w["ct"], w["pp"], w["qq"], w["d_skip"], s5_h0,
                                    nb=nb, seq=seq, tl=tl)
    yb, lru_new, conv_new = _lru_branch(z, w["conv_w"], w["conv_b"], w["wri"], w["b_r"],
                                        w["b_i"], w["lru_lam"], lru_h0, conv_buf,
                                        nb=nb, seq=seq, tl=tl)
    ya2 = _glu(ya_b, ya_f, w["w_glu"], w["b_glu"], tm=tm, tn=512)
    merged = _merge(ya2, w["p_a"], yb, w["p_b"], z, tm=min(512, tm), tn=512)
    x = _outnorm(merged, w["w_out"], x, w["g_post_mix"], tm=tm, tn=512)
    h = _rmsnorm(x, w["g_pre_ffn"])
    hmid = _ffn_up(h, w["w_gate"], w["w_up"], tm=tm, tn=256)
    x = _outnorm(hmid, w["w_down"], x, w["g_post_ffn"], tm=min(512, tm), tn=256)
    return x, s5_new, lru_new, conv_new


def _to_rows(x):
    n, l, d = x.shape
    return x.transpose(1, 0, 2).reshape(l * n, d)


def _from_rows(x, n):
    t, d = x.shape
    return x.reshape(t // n, n, d).transpose(1, 0, 2)


def kernel(x_prompt, x_sample, state_s5_re, state_s5_im, state_lru, cache_conv, g_pre_mix, w_in, s5_lam_re, s5_lam_im, s5_log_step, s5_b_re, s5_b_im, s5_c_re, s5_c_im, s5_d, w_glu, b_glu, conv_w, conv_b, lru_w_r, lru_b_r, lru_w_i, lru_b_i, lru_lam, p_a, p_b, w_out, g_post_mix, g_pre_ffn, w_gate, w_up, w_down, g_post_ffn):
    depth = w_in.shape[0]
    nbp, lp, _ = x_prompt.shape
    nbs, ls, _ = x_sample.shape
    sd = state_lru.dtype

    kb, ct, pp, qq = _s5_prep(s5_lam_re, s5_lam_im, s5_log_step,
                              s5_b_re.transpose(0, 1, 3, 2), s5_b_im.transpose(0, 1, 3, 2),
                              s5_c_re, s5_c_im)
    wri = jnp.concatenate([lru_w_r, lru_w_i], axis=-1).astype(BF16)

    xp = _to_rows(x_prompt)
    xs = _to_rows(x_sample)
    outs_p, outs_s = [], []
    for l in range(depth):
        row = lambda a: a[l].reshape(1, -1)
        w = dict(
            g_pre_mix=row(g_pre_mix), w_in=w_in[l].astype(BF16),
            kb=kb[l], ct=ct[l], pp=pp[l], qq=qq[l], d_skip=row(s5_d),
            w_glu=w_glu[l].astype(BF16), b_glu=row(b_glu),
            conv_w=conv_w[l], conv_b=row(conv_b), wri=wri[l],
            b_r=row(lru_b_r), b_i=row(lru_b_i), lru_lam=row(lru_lam),
            p_a=p_a[l].astype(BF16), p_b=p_b[l].astype(BF16), w_out=w_out[l].astype(BF16),
            g_post_mix=row(g_post_mix), g_pre_ffn=row(g_pre_ffn),
            w_gate=w_gate[l].astype(BF16), w_up=w_up[l].astype(BF16),
            w_down=w_down[l].astype(BF16), g_post_ffn=row(g_post_ffn))

        xp, s5p, lrup, convp = _layer(
            xp, jnp.zeros((N_GROUPS, nbp, 2 * P_STATE), F32), jnp.zeros((nbp, D_MODEL), F32),
            jnp.zeros(((CONV_W - 1) * nbp, D_MODEL), F32), w, nb=nbp, seq=lp, tl=512)
        outs_p.append((s5p, lrup, convp))

        s5_h0 = jnp.concatenate([state_s5_re[l], state_s5_im[l]], axis=-1).transpose(1, 0, 2)
        cbuf = cache_conv[l].transpose(1, 0, 2).reshape((CONV_W - 1) * nbs, D_MODEL)
        xs, s5s, lrus, convs = _layer(xs, s5_h0.astype(F32), state_lru[l].astype(F32),
                                      cbuf.astype(F32), w, nb=nbs, seq=ls, tl=ls)
        outs_s.append((s5s, lrus, convs))

    def pack(outs, nb):
        s5 = jnp.stack([o[0] for o in outs]).transpose(0, 2, 1, 3)
        lru = jnp.stack([o[1] for o in outs])
        conv = jnp.stack([o[2] for o in outs]).reshape(depth, CONV_W - 1, nb, D_MODEL)
        return (s5[..., :P_STATE].astype(sd), s5[..., P_STATE:].astype(sd), lru.astype(sd),
                conv.transpose(0, 2, 1, 3).astype(sd))

    p_re, p_im, p_lru, p_conv = pack(outs_p, nbp)
    s_re, s_im, s_lru, s_conv = pack(outs_s, nbs)
    return (_from_rows(xp, nbp), _from_rows(xs, nbs), p_re, p_im, p_lru, p_conv,
            s_re, s_im, s_lru, s_conv)
```

```python
import functools
import math

import jax
import jax.numpy as jnp
from jax import lax
from jax.experimental import pallas as pl
from jax.experimental.pallas import tpu as pltpu

F32 = jnp.float32
BF16 = jnp.bfloat16

D_MODEL = 4096
S5_GROUP = 16
N_GROUPS = D_MODEL // S5_GROUP
P_STATE = 64
S5_CHUNK = 16
N_BLOCKS_B = 16
BLOCK_B = D_MODEL // N_BLOCKS_B
CONV_W = 4
LRU_C = 8.0
D_FF = 11008
EPS = 1e-6
LANES = 128
MIB = 1024 * 1024


def _cparams(sem, vmem_mib):
    return pltpu.CompilerParams(dimension_semantics=sem, vmem_limit_bytes=vmem_mib * MIB)


def _sigmoid(x):
    return 0.5 * (jnp.tanh(0.5 * x) + 1.0)


def _gelu(x):
    c = math.sqrt(2.0 / math.pi)
    return 0.5 * x * (1.0 + jnp.tanh(c * (x + 0.044715 * (x * x * x))))


def _dot(a, b):
    return jnp.dot(a, b, preferred_element_type=F32)


def _dot_nt(a, b, precision=None):
    return lax.dot_general(a, b, (((1,), (1,)), ((), ())), precision=precision,
                           preferred_element_type=F32)


def _any_spec():
    return pl.BlockSpec(memory_space=pl.ANY)


def _s5_prep_kernel(lre_ref, lim_ref, ls_ref, bre_ref, bim_ref, cre_ref, cim_ref,
                    kb_ref, ct_ref, p_ref, q_ref, *, gb):
    tau = lax.broadcasted_iota(jnp.int32, (24, P_STATE), 0).astype(F32)
    lane = lax.broadcasted_iota(jnp.int32, (S5_GROUP, 2 * LANES), 1)
    for g in range(gb):
        lr = lre_ref[0, g:g + 1, :]
        li = lim_ref[0, g:g + 1, :]
        step = jnp.exp(ls_ref[0, g:g + 1, :])
        xr = lr * step
        xi = li * step
        mag = jnp.exp(xr * tau)
        ang = xi * tau
        pw_r = mag * jnp.cos(ang)
        pw_i = mag * jnp.sin(ang)
        a_r = pw_r[1:2]
        a_i = pw_i[1:2]
        den = lr * lr + li * li
        n_r = a_r - 1.0
        n_i = a_i
        co_r = (n_r * lr + n_i * li) / den
        co_i = (n_i * lr - n_r * li) / den
        bt_r = bre_ref[0, g]
        bt_i = bim_ref[0, g]
        bb_r = co_r * bt_r - co_i * bt_i
        bb_i = co_r * bt_i + co_i * bt_r
        c_r = cre_ref[0, g]
        c_i = cim_ref[0, g]
        ca = []
        for t in range(S5_CHUNK + 1):
            e_r = pw_r[t:t + 1]
            e_i = pw_i[t:t + 1]
            re = c_r * e_r - c_i * e_i
            im = c_r * e_i + c_i * e_r
            ca.append(jnp.concatenate([re, -im], axis=1))
        ct_ref[0, g] = jnp.concatenate(ca[1:], axis=0).T.astype(BF16)
        ca_all = jnp.concatenate(ca[:S5_CHUNK], axis=0)
        b2 = jnp.concatenate([bb_r, bb_i], axis=1)
        strip = _dot_nt(b2, ca_all, precision=lax.Precision.HIGHEST)
        for s in range(S5_CHUNK):
            if s == 0:
                blk = strip
            else:
                blk = jnp.where(lane >= 16 * s, pltpu.roll(strip, 16 * s, 1), 0.0)
            kb_ref[0, g, s * 16:(s + 1) * 16, 0:256] = blk.astype(BF16)
            e_r = pw_r[15 - s:16 - s]
            e_i = pw_i[15 - s:16 - s]
            bm_r = bb_r * e_r - bb_i * e_i
            bm_i = bb_r * e_i + bb_i * e_r
            kb_ref[0, g, s * 16:(s + 1) * 16, 256:384] = jnp.concatenate(
                [bm_r, bm_i], axis=1).astype(BF16)
        a16_r = pw_r[16:17]
        a16_i = pw_i[16:17]
        p_ref[0, g:g + 1, :] = jnp.concatenate([a16_r, a16_r], axis=1)
        q_ref[0, g:g + 1, :] = jnp.concatenate([-a16_i, a16_i], axis=1)


def _s5_prep(lam_re, lam_im, log_step, bt_re, bt_im, c_re, c_im):
    depth = lam_re.shape[0]
    gb = 8
    grid = (depth, N_GROUPS // gb)
    m3 = lambda l, g: (l, g, 0)
    m4 = lambda l, g: (l, g, 0, 0)
    return pl.pallas_call(
        functools.partial(_s5_prep_kernel, gb=gb),
        grid=grid,
        in_specs=[
            pl.BlockSpec((1, gb, P_STATE), m3),
            pl.BlockSpec((1, gb, P_STATE), m3),
            pl.BlockSpec((1, gb, 1), m3),
            pl.BlockSpec((1, gb, S5_GROUP, P_STATE), m4),
            pl.BlockSpec((1, gb, S5_GROUP, P_STATE), m4),
            pl.BlockSpec((1, gb, S5_GROUP, P_STATE), m4),
            pl.BlockSpec((1, gb, S5_GROUP, P_STATE), m4),
        ],
        out_specs=[
            pl.BlockSpec((1, gb, 256, 384), m4),
            pl.BlockSpec((1, gb, 128, 256), m4),
            pl.BlockSpec((1, gb, 128), m3),
            pl.BlockSpec((1, gb, 128), m3),
        ],
        out_shape=[
            jax.ShapeDtypeStruct((depth, N_GROUPS, 256, 384), BF16),
            jax.ShapeDtypeStruct((depth, N_GROUPS, 128, 256), BF16),
            jax.ShapeDtypeStruct((depth, N_GROUPS, 128), F32),
            jax.ShapeDtypeStruct((depth, N_GROUPS, 128), F32),
        ],
        compiler_params=_cparams(("arbitrary", "arbitrary"), 32),
        name="s5_prep",
    )(lam_re, lam_im, log_step.reshape(depth, N_GROUPS, 1), bt_re, bt_im, c_re, c_im)


def _block_transpose8(arrs, masks):
    arrs = list(arrs)
    for k in range(3):
        d = 1 << k
        sh = 16 * d
        new = list(arrs)
        for a in range(8):
            if a & d:
                continue
            b = a | d
            va, vb = arrs[a], arrs[b]
            new[a] = jnp.where(masks[k], pltpu.roll(vb, sh, 1), va)
            new[b] = jnp.where(masks[k], vb, pltpu.roll(va, LANES - sh, 1))
        arrs = new
    return arrs


def _lane_block_masks(rows):
    lane_blk = lax.broadcasted_iota(jnp.int32, (rows, LANES), 1) // 16
    return [((lane_blk >> k) & 1) == 1 for k in range(3)]


def _s5_kernel(u_ref, kb_ref, ct_ref, p_ref, q_ref, d_ref, h0_ref, *rest, nb, tl, gps, aliased):
    if aliased:
        rest = rest[2:]
    yf_ref, yb_ref, hout_ref, uflat, hb, hbs, hprev, ybuf, carry = rest
    i = pl.program_id(1)
    nt = pl.num_programs(1)
    nc = tl // S5_CHUNK
    cb = min(32, nc)
    nsub = nc // cb
    sr = cb * nb
    nh = gps // 8

    @pl.when(i == 0)
    def _():
        carry[...] = h0_ref[...]

    masks_f = _lane_block_masks(sr)
    masks_p = _lane_block_masks(sr // 2)

    def fwd(j, _):
        base = pl.multiple_of(j * (cb * S5_CHUNK * nb), 8)
        r0 = pl.multiple_of(j * sr, 16)
        for hh in range(nh):
            for sh in range(2):
                arrs = []
                for sl in range(8):
                    s = sh * 8 + sl
                    pieces = [u_ref[pl.ds(base + (cc * S5_CHUNK + s) * nb, nb),
                                    hh * LANES:(hh + 1) * LANES] for cc in range(cb)]
                    v = pieces[0] if cb == 1 else jnp.concatenate(pieces, axis=0)
                    arrs.append(pltpu.bitcast(v.astype(BF16), jnp.uint32))
                w = _block_transpose8(arrs, masks_p)
                for g in range(8):
                    uflat[hh * 8 + g, pl.ds(r0, sr), sh * LANES:(sh + 1) * LANES] = (
                        pltpu.bitcast(w[g], BF16))
        return 0

    lax.fori_loop(0, nsub, fwd, 0)

    for gi in range(gps):
        hbg = _dot(uflat[gi], kb_ref[gi, :, 256:384])
        hb[gi] = hbg
        hbs[gi] = pltpu.roll(hbg, P_STATE, 1)

    def step(c, hs):
        rows = pl.ds(pl.multiple_of(c * nb, 8), nb)
        out = []
        for gi in range(gps):
            h, hsw = hs[2 * gi], hs[2 * gi + 1]
            hprev[gi, rows, :] = h
            pg = p_ref[gi:gi + 1, :]
            qg = q_ref[gi:gi + 1, :]
            out.append(pg * h + qg * hsw + hb[gi, rows, :])
            out.append(pg * hsw - qg * h + hbs[gi, rows, :])
        return tuple(out)

    init = []
    for gi in range(gps):
        h = carry[gi]
        init += [h, pltpu.roll(h, P_STATE, 1)]
    hs = lax.fori_loop(0, nc, step, tuple(init))
    for gi in range(gps):
        carry[gi] = hs[2 * gi]

    for gi in range(gps):
        ybuf[gi] = (_dot(uflat[gi], kb_ref[gi, :, 0:256])
                    + _dot(hprev[gi].astype(BF16), ct_ref[gi]))

    def inv(j, _):
        base = pl.multiple_of(j * (cb * S5_CHUNK * nb), 8)
        r0 = pl.multiple_of(j * sr, 16)
        for hh in range(nh):
            lanes = slice(hh * LANES, (hh + 1) * LANES)
            dsk = d_ref[:, lanes]
            for sh in range(2):
                arrs = [ybuf[hh * 8 + g, pl.ds(r0, sr), sh * LANES:(sh + 1) * LANES]
                        for g in range(8)]
                z = _block_transpose8(arrs, masks_f)
                for sl in range(0, 8, 2):
                    t = sh * 8 + sl
                    for cc in range(cb):
                        rows = pl.ds(base + (cc * S5_CHUNK + t) * nb, 2 * nb)
                        y2 = jnp.concatenate([z[sl][cc * nb:(cc + 1) * nb, :],
                                              z[sl + 1][cc * nb:(cc + 1) * nb, :]], axis=0)
                        ya = _gelu(y2 + dsk * u_ref[rows, lanes])
                        yf_ref[rows, lanes] = ya
                        yb_ref[rows, lanes] = ya.astype(BF16)
        return 0

    lax.fori_loop(0, nsub, inv, 0)

    @pl.when(i == nt - 1)
    def _():
        hout_ref[...] = carry[...]


def _s5_branch(z, kb, ct, pp, qq, dskip, h0, prev, *, layer, nb, seq, tl, row0):
    gps = 16
    tm = tl * nb
    nt = seq // tl
    nblk = N_GROUPS // gps
    cw = gps * S5_GROUP
    r = (tl // S5_CHUNK) * nb
    rb0 = row0 // tm
    t_rows = z.shape[0]
    aliased = prev is not None
    in_specs = [
        pl.BlockSpec((tm, cw), lambda b, i: (rb0 + i, b)),
        pl.BlockSpec((None, gps, 256, 384), lambda b, i: (layer, b, 0, 0)),
        pl.BlockSpec((None, gps, 128, 256), lambda b, i: (layer, b, 0, 0)),
        pl.BlockSpec((None, gps, 128), lambda b, i: (layer, b, 0)),
        pl.BlockSpec((None, gps, 128), lambda b, i: (layer, b, 0)),
        pl.BlockSpec((None, 1, cw), lambda b, i: (layer, 0, b)),
        pl.BlockSpec((gps, nb, 128), lambda b, i: (b, 0, 0)),
    ]
    args = [z, kb, ct, pp, qq, dskip, h0]
    aliases = {}
    if aliased:
        in_specs += [_any_spec(), _any_spec()]
        args += list(prev)
        aliases = {7: 0, 8: 1}
    return pl.pallas_call(
        functools.partial(_s5_kernel, nb=nb, tl=tl, gps=gps, aliased=aliased),
        grid=(nblk, nt),
        in_specs=in_specs,
        out_specs=[
            pl.BlockSpec((tm, cw), lambda b, i: (rb0 + i, b)),
            pl.BlockSpec((tm, cw), lambda b, i: (rb0 + i, b)),
            pl.BlockSpec((gps, nb, 128), lambda b, i: (b, 0, 0)),
        ],
        out_shape=[
            jax.ShapeDtypeStruct((t_rows, D_MODEL), F32),
            jax.ShapeDtypeStruct((t_rows, D_MODEL), BF16),
            jax.ShapeDtypeStruct((N_GROUPS, nb, 128), F32),
        ],
        scratch_shapes=[
            pltpu.VMEM((gps, r, 256), BF16),
            pltpu.VMEM((gps, r, 128), F32),
            pltpu.VMEM((gps, r, 128), F32),
            pltpu.VMEM((gps, r, 128), F32),
            pltpu.VMEM((gps, r, 256), F32),
            pltpu.VMEM((gps, nb, 128), F32),
        ],
        input_output_aliases=aliases,
        compiler_params=_cparams(("arbitrary", "arbitrary"), 58),
        name="s5_branch",
    )(*args)


def _lru_kernel(xb_ref, gt_ref, cw_ref, cb_ref, wri_ref, br_ref, bi_ref, lam_ref,
                h0_ref, cbuf_ref, *rest, nb, tl, rc, aliased):
    if aliased:
        rest = rest[1:]
    y_ref, hout_ref, cout_ref, xs, a_s, b_s, hc = rest
    i = pl.program_id(1)
    nt = pl.num_programs(1)
    tm = tl * nb
    hist = (CONV_W - 1) * nb

    @pl.when(i == 0)
    def _():
        hc[...] = h0_ref[...]
        xs[0:hist, :] = cbuf_ref[...]

    @pl.when(i > 0)
    def _():
        xs[0:hist, :] = xs[tm:tm + hist, :]

    xs[hist:hist + tm, :] = xb_ref[...]

    cw = cw_ref[...]
    cbias = cb_ref[...]
    br = br_ref[...]
    bi = bi_ref[...]
    nlam = -lam_ref[...]
    sp = jnp.maximum(nlam, 0.0) + jnp.log1p(jnp.exp(-jnp.abs(nlam)))

    def gates(c, _):
        r0 = pl.multiple_of(c * rc, rc)
        acc = cw[0:1] * xs[pl.ds(r0, rc), :]
        for k in range(1, CONV_W):
            acc = acc + cw[k:k + 1] * xs[pl.ds(r0 + k * nb, rc), :]
        xc = cbias + acc
        gm = _dot(xc.astype(BF16), wri_ref[...])
        r = _sigmoid(gm[:, 0:BLOCK_B] + br)
        ig = _sigmoid(gm[:, BLOCK_B:2 * BLOCK_B] + bi)
        log_a = -LRU_C * r * sp
        a = jnp.exp(log_a)
        a_s[pl.ds(r0, rc), :] = a
        b_s[pl.ds(r0, rc), :] = jnp.sqrt(-jnp.tanh(log_a) * (a * a + 1.0)) * (ig * xc)
        return 0

    lax.fori_loop(0, tm // rc, gates, 0)

    def scan(t, h):
        rows = pl.ds(pl.multiple_of(t * nb, 8), nb)
        h = a_s[rows, :] * h + b_s[rows, :]
        b_s[rows, :] = h
        return h

    h = lax.fori_loop(0, tl, scan, hc[...], unroll=8)
    hc[...] = h

    def outp(c, _):
        rows = pl.ds(pl.multiple_of(c * rc, rc), rc)
        y_ref[rows, :] = (b_s[rows, :] * _gelu(gt_ref[rows, :])).astype(BF16)
        return 0

    lax.fori_loop(0, tm // rc, outp, 0)

    @pl.when(i == nt - 1)
    def _():
        hout_ref[...] = h
        cout_ref[...] = xs[tm:tm + hist, :]


def _lru_branch(z, conv_w, conv_b, wri, b_r, b_i, lam, h0, cbuf, prev, *, layer, nb, seq, tl,
                row0):
    tm = tl * nb
    nt = seq // tl
    rc = min(256, tm)
    hist = (CONV_W - 1) * nb
    xoff = D_MODEL // BLOCK_B
    rb0 = row0 // tm
    aliased = prev is not None
    col = lambda k, i: (0, k)
    lcol = lambda k, i: (layer, 0, k)
    in_specs = [
        pl.BlockSpec((tm, BLOCK_B), lambda k, i: (rb0 + i, xoff + k)),
        pl.BlockSpec((tm, BLOCK_B), lambda k, i: (rb0 + i, 2 * xoff + k)),
        pl.BlockSpec((None, CONV_W, BLOCK_B), lcol),
        pl.BlockSpec((None, 1, BLOCK_B), lcol),
        pl.BlockSpec((None, None, BLOCK_B, 2 * BLOCK_B), lambda k, i: (layer, k, 0, 0)),
        pl.BlockSpec((None, 1, BLOCK_B), lcol),
        pl.BlockSpec((None, 1, BLOCK_B), lcol),
        pl.BlockSpec((None, 1, BLOCK_B), lcol),
        pl.BlockSpec((nb, BLOCK_B), col),
        pl.BlockSpec((hist, BLOCK_B), col),
    ]
    args = [z, z, conv_w, conv_b, wri, b_r, b_i, lam, h0, cbuf]
    aliases = {}
    if aliased:
        in_specs.append(_any_spec())
        args.append(prev)
        aliases = {10: 0}
    return pl.pallas_call(
        functools.partial(_lru_kernel, nb=nb, tl=tl, rc=rc, aliased=aliased),
        grid=(N_BLOCKS_B, nt),
        in_specs=in_specs,
        out_specs=[
            pl.BlockSpec((tm, BLOCK_B), lambda k, i: (rb0 + i, k)),
            pl.BlockSpec((nb, BLOCK_B), col),
            pl.BlockSpec((hist, BLOCK_B), col),
        ],
        out_shape=[
            jax.ShapeDtypeStruct((z.shape[0], D_MODEL), BF16),
            jax.ShapeDtypeStruct((nb, D_MODEL), F32),
            jax.ShapeDtypeStruct((hist, D_MODEL), F32),
        ],
        scratch_shapes=[
            pltpu.VMEM((tm + hist, BLOCK_B), F32),
            pltpu.VMEM((tm, BLOCK_B), F32),
            pltpu.VMEM((tm, BLOCK_B), F32),
            pltpu.VMEM((nb, BLOCK_B), F32),
        ],
        input_output_aliases=aliases,
        compiler_params=_cparams(("arbitrary", "arbitrary"), 48),
        name="lru_branch",
    )(*args)


def _rmsnorm_kernel(x_ref, g_ref, o_ref):
    x = x_ref[...]
    inv = lax.rsqrt(jnp.mean(x * x, axis=-1, keepdims=True) + EPS)
    o_ref[...] = (x * inv * g_ref[...]).astype(o_ref.dtype)


def _rmsnorm(x, g, *, layer, tr):
    t = x.shape[0]
    return pl.pallas_call(
        _rmsnorm_kernel,
        grid=(t // tr,),
        in_specs=[pl.BlockSpec((tr, D_MODEL), lambda i: (i, 0)),
                  pl.BlockSpec((None, 1, D_MODEL), lambda i: (layer, 0, 0))],
        out_specs=pl.BlockSpec((tr, D_MODEL), lambda i: (i, 0)),
        out_shape=jax.ShapeDtypeStruct((t, D_MODEL), BF16),
        compiler_params=_cparams(("arbitrary",), 40),
        name="rmsnorm",
    )(x, g)


def _finish_kernel(x_ref, f_ref, g_ref, gn_ref, xo_ref, h_ref):
    f = f_ref[...]
    inv = lax.rsqrt(jnp.mean(f * f, axis=-1, keepdims=True) + EPS)
    x = x_ref[...] + f * inv * g_ref[...]
    xo_ref[...] = x
    inv2 = lax.rsqrt(jnp.mean(x * x, axis=-1, keepdims=True) + EPS)
    h_ref[...] = (x * inv2 * gn_ref[...]).astype(h_ref.dtype)


def _finish_last_kernel(x_ref, f_ref, g_ref, xo_ref):
    f = f_ref[...]
    inv = lax.rsqrt(jnp.mean(f * f, axis=-1, keepdims=True) + EPS)
    xo_ref[...] = x_ref[...] + f * inv * g_ref[...]


def _finish(x, f, g, layer, g_next, layer_next, *, tr):
    t = x.shape[0]
    row = pl.BlockSpec((tr, D_MODEL), lambda i: (i, 0))
    vec = lambda l: pl.BlockSpec((None, 1, D_MODEL), lambda i: (l, 0, 0))
    if g_next is None:
        return pl.pallas_call(
            _finish_last_kernel,
            grid=(t // tr,),
            in_specs=[row, row, vec(layer)],
            out_specs=row,
            out_shape=jax.ShapeDtypeStruct((t, D_MODEL), F32),
            compiler_params=_cparams(("arbitrary",), 40),
            name="finish_last",
        )(x, f, g), None
    return pl.pallas_call(
        _finish_kernel,
        grid=(t // tr,),
        in_specs=[row, row, vec(layer), vec(layer_next)],
        out_specs=[row, row],
        out_shape=[jax.ShapeDtypeStruct((t, D_MODEL), F32),
                   jax.ShapeDtypeStruct((t, D_MODEL), BF16)],
        compiler_params=_cparams(("arbitrary",), 48),
        name="finish",
    )(x, f, g, g_next)


def _mm_kernel(a_ref, w_ref, o_ref):
    o_ref[...] = _dot(a_ref[...], w_ref[...]).astype(o_ref.dtype)


def _matmul(a, w, *, layer, tm, tn, vmem_mib=48):
    t, k = a.shape
    n = w.shape[2]
    return pl.pallas_call(
        _mm_kernel,
        grid=(t // tm, n // tn),
        in_specs=[pl.BlockSpec((tm, k), lambda i, j: (i, 0)),
                  pl.BlockSpec((None, k, tn), lambda i, j: (layer, 0, j))],
        out_specs=pl.BlockSpec((tm, tn), lambda i, j: (i, j)),
        out_shape=jax.ShapeDtypeStruct((t, n), F32),
        compiler_params=_cparams(("arbitrary", "arbitrary"), vmem_mib),
        name="matmul",
    )(a, w)


def _glu_kernel(a_ref, w_ref, b_ref, y_ref, o_ref):
    gate = _sigmoid(_dot(a_ref[...], w_ref[...]) + b_ref[...])
    o_ref[...] = (y_ref[...] * gate).astype(o_ref.dtype)


def _glu(ya_b, ya_f, w, b, *, layer, tm, tn):
    t = ya_b.shape[0]
    return pl.pallas_call(
        _glu_kernel,
        grid=(t // tm, D_MODEL // tn),
        in_specs=[pl.BlockSpec((tm, D_MODEL), lambda i, j: (i, 0)),
                  pl.BlockSpec((None, D_MODEL, tn), lambda i, j: (layer, 0, j)),
                  pl.BlockSpec((None, 1, tn), lambda i, j: (layer, 0, j)),
                  pl.BlockSpec((tm, tn), lambda i, j: (i, j))],
        out_specs=pl.BlockSpec((tm, tn), lambda i, j: (i, j)),
        out_shape=jax.ShapeDtypeStruct((t, D_MODEL), BF16),
        compiler_params=_cparams(("arbitrary", "arbitrary"), 48),
        name="glu",
    )(ya_b, w, b, ya_f)


def _merge_kernel(a1_ref, w1_ref, a2_ref, w2_ref, ga_ref, gb_ref, o_ref):
    m = (_sigmoid(ga_ref[...]) * _dot(a1_ref[...], w1_ref[...])
         + _sigmoid(gb_ref[...]) * _dot(a2_ref[...], w2_ref[...]))
    o_ref[...] = m.astype(o_ref.dtype)


def _merge(ya, p_a, yb, p_b, z, *, layer, tm, tn):
    t = ya.shape[0]
    ga_off = 3 * D_MODEL // tn
    gb_off = 4 * D_MODEL // tn
    wspec = pl.BlockSpec((None, D_MODEL, tn), lambda i, j: (layer, 0, j))
    return pl.pallas_call(
        _merge_kernel,
        grid=(t // tm, D_MODEL // tn),
        in_specs=[pl.BlockSpec((tm, D_MODEL), lambda i, j: (i, 0)),
                  wspec,
                  pl.BlockSpec((tm, D_MODEL), lambda i, j: (i, 0)),
                  wspec,
                  pl.BlockSpec((tm, tn), lambda i, j: (i, ga_off + j)),
                  pl.BlockSpec((tm, tn), lambda i, j: (i, gb_off + j))],
        out_specs=pl.BlockSpec((tm, tn), lambda i, j: (i, j)),
        out_shape=jax.ShapeDtypeStruct((t, D_MODEL), BF16),
        compiler_params=_cparams(("arbitrary", "arbitrary"), 56),
        name="merge",
    )(ya, p_a, yb, p_b, z, z)


def _ffn_up_kernel(a_ref, wg_ref, wu_ref, o_ref):
    a = a_ref[...]
    g = _dot(a, wg_ref[...])
    u = _dot(a, wu_ref[...])
    o_ref[...] = (g * _sigmoid(g) * u).astype(o_ref.dtype)


def _ffn_up(h, wg, wu, *, layer, tm, tn):
    t = h.shape[0]
    wspec = pl.BlockSpec((None, D_MODEL, tn), lambda i, j: (layer, 0, j))
    return pl.pallas_call(
        _ffn_up_kernel,
        grid=(t // tm, D_FF // tn),
        in_specs=[pl.BlockSpec((tm, D_MODEL), lambda i, j: (i, 0)), wspec, wspec],
        out_specs=pl.BlockSpec((tm, tn), lambda i, j: (i, j)),
        out_shape=jax.ShapeDtypeStruct((t, D_FF), BF16),
        compiler_params=_cparams(("arbitrary", "arbitrary"), 48),
        name="ffn_up",
    )(h, wg, wu)


def _to_rows(x):
    n, l, d = x.shape
    return x.transpose(1, 0, 2).reshape(l * n, d)


def _from_rows(x, n):
    t, d = x.shape
    return x.reshape(t // n, n, d).transpose(1, 0, 2)


def kernel(x_prompt, x_sample, state_s5_re, state_s5_im, state_lru, cache_conv, g_pre_mix, w_in, s5_lam_re, s5_lam_im, s5_log_step, s5_b_re, s5_b_im, s5_c_re, s5_c_im, s5_d, w_glu, b_glu, conv_w, conv_b, lru_w_r, lru_b_r, lru_w_i, lru_b_i, lru_lam, p_a, p_b, w_out, g_post_mix, g_pre_ffn, w_gate, w_up, w_down, g_post_ffn):
    depth = w_in.shape[0]
    nbp, lp, _ = x_prompt.shape
    nbs, ls, _ = x_sample.shape
    sd = state_lru.dtype
    rows_p = nbp * lp
    rows_s = nbs * ls
    hist_p = (CONV_W - 1) * nbp
    hist_s = (CONV_W - 1) * nbs
    tm = 1280 if (rows_p + rows_s) % 1280 == 0 else 256
    tm_half = tm // 2
    tl_p = min(512, lp)

    kb, ct, pp, qq = _s5_prep(s5_lam_re, s5_lam_im, s5_log_step,
                              s5_b_re.transpose(0, 1, 3, 2), s5_b_im.transpose(0, 1, 3, 2),
                              s5_c_re, s5_c_im)
    vec = lambda a: a.reshape(depth, 1, -1)
    w_in_b = w_in.astype(BF16)
    w_glu_b = w_glu.astype(BF16)
    p_a_b = p_a.astype(BF16)
    p_b_b = p_b.astype(BF16)
    w_out_b = w_out.astype(BF16)
    w_gate_b = w_gate.astype(BF16)
    w_up_b = w_up.astype(BF16)
    w_down_b = w_down.astype(BF16)
    wri = jnp.concatenate([lru_w_r, lru_w_i], axis=-1).astype(BF16)
    g_pre_mix, g_post_mix, g_pre_ffn, g_post_ffn = map(
        vec, (g_pre_mix, g_post_mix, g_pre_ffn, g_post_ffn))
    s5_d, b_glu, conv_b, lru_b_r, lru_b_i, lru_lam = map(
        vec, (s5_d, b_glu, conv_b, lru_b_r, lru_b_i, lru_lam))

    x = jnp.concatenate([_to_rows(x_prompt), _to_rows(x_sample)], axis=0)
    h = _rmsnorm(x, g_pre_mix, layer=0, tr=256)
    zeros_s5 = jnp.zeros((N_GROUPS, nbp, 2 * P_STATE), F32)
    zeros_lru = jnp.zeros((nbp, D_MODEL), F32)
    zeros_conv = jnp.zeros((hist_p, D_MODEL), F32)
    outs_p, outs_s = [], []
    for l in range(depth):
        z = _matmul(h, w_in_b, layer=l, tm=tm, tn=512)

        ya_f, ya_b, s5p = _s5_branch(z, kb, ct, pp, qq, s5_d, zeros_s5, None,
                                     layer=l, nb=nbp, seq=lp, tl=tl_p, row0=0)
        s5_h0 = jnp.concatenate([state_s5_re[l], state_s5_im[l]], axis=-1).transpose(1, 0, 2)
        ya_f, ya_b, s5s = _s5_branch(z, kb, ct, pp, qq, s5_d, s5_h0.astype(F32), (ya_f, ya_b),
                                     layer=l, nb=nbs, seq=ls, tl=ls, row0=rows_p)
        lru_args = (conv_w, conv_b, wri, lru_b_r, lru_b_i, lru_lam)
        yb, lrup, convp = _lru_branch(z, *lru_args, zeros_lru, zeros_conv, None,
                                      layer=l, nb=nbp, seq=lp, tl=tl_p, row0=0)
        cbuf = cache_conv[l].transpose(1, 0, 2).reshape(hist_s, D_MODEL)
        yb, lrus, convs = _lru_branch(z, *lru_args, state_lru[l].astype(F32), cbuf.astype(F32),
                                      yb, layer=l, nb=nbs, seq=ls, tl=ls, row0=rows_p)
        outs_p.append((s5p, lrup, convp))
        outs_s.append((s5s, lrus, convs))

        ya2 = _glu(ya_b, ya_f, w_glu_b, b_glu, layer=l, tm=tm, tn=512)
        merged = _merge(ya2, p_a_b, yb, p_b_b, z, layer=l, tm=tm_half, tn=512)
        f = _matmul(merged, w_out_b, layer=l, tm=tm, tn=512)
        x, h = _finish(x, f, g_post_mix, l, g_pre_ffn, l, tr=256)
        hmid = _ffn_up(h, w_gate_b, w_up_b, layer=l, tm=tm, tn=256)
        f = _matmul(hmid, w_down_b, layer=l, tm=tm_half, tn=256)
        if l + 1 < depth:
            x, h = _finish(x, f, g_post_ffn, l, g_pre_mix, l + 1, tr=256)
        else:
            x, _ = _finish(x, f, g_post_ffn, l, None, None, tr=256)

    def pack(outs, nb):
        s5 = jnp.stack([o[0] for o in outs]).transpose(0, 2, 1, 3)
        lru = jnp.stack([o[1] for o in outs])
        conv = jnp.stack([o[2] for o in outs]).reshape(depth, CONV_W - 1, nb, D_MODEL)
        return (s5[..., :P_STATE].astype(sd), s5[..., P_STATE:].astype(sd), lru.astype(sd),
                conv.transpose(0, 2, 1, 3).astype(sd))

    p_re, p_im, p_lru, p_conv = pack(outs_p, nbp)
    s_re, s_im, s_lru, s_conv = pack(outs_s, nbs)
    return (_from_rows(x[:rows_p], nbp), _from_rows(x[rows_p:], nbs), p_re, p_im, p_lru, p_conv,
            s_re, s_im, s_lru, s_conv)
```

```python
import functools
import math

import jax
import jax.numpy as jnp
from jax import lax
from jax.experimental import pallas as pl
from jax.experimental.pallas import tpu as pltpu

F32 = jnp.float32
BF16 = jnp.bfloat16

D_MODEL = 4096
S5_GROUP = 16
N_GROUPS = D_MODEL // S5_GROUP
P_STATE = 64
S5_CHUNK = 16
N_BLOCKS_B = 16
BLOCK_B = D_MODEL // N_BLOCKS_B
CONV_W = 4
LRU_C = 8.0
D_FF = 11008
EPS = 1e-6
LANES = 128
MIB = 1024 * 1024


def _cparams(sem, vmem_mib):
    return pltpu.CompilerParams(dimension_semantics=sem, vmem_limit_bytes=vmem_mib * MIB)


def _sigmoid(x):
    return 0.5 * (jnp.tanh(0.5 * x) + 1.0)


def _gelu(x):
    c = math.sqrt(2.0 / math.pi)
    hx = 0.5 * x
    return hx + hx * jnp.tanh(x * (c + (c * 0.044715) * (x * x)))


def _sqrt_nonneg(v):
    return jnp.where(v == 0.0, 0.0, v * lax.rsqrt(v))


def _dot(a, b):
    return jnp.dot(a, b, preferred_element_type=F32)


def _dot_nt(a, b, precision=None):
    return lax.dot_general(a, b, (((1,), (1,)), ((), ())), precision=precision,
                           preferred_element_type=F32)


def _any_spec():
    return pl.BlockSpec(memory_space=pl.ANY)


def _s5_prep_kernel(lre_ref, lim_ref, ls_ref, bre_ref, bim_ref, cre_ref, cim_ref,
                    kb_ref, ct_ref, p_ref, q_ref, *, gb):
    tau = lax.broadcasted_iota(jnp.int32, (24, P_STATE), 0).astype(F32)
    lane = lax.broadcasted_iota(jnp.int32, (S5_GROUP, 2 * LANES), 1)
    for g in range(gb):
        lr = lre_ref[0, g:g + 1, :]
        li = lim_ref[0, g:g + 1, :]
        step = jnp.exp(ls_ref[0, g:g + 1, :])
        xr = lr * step
        xi = li * step
        mag = jnp.exp(xr * tau)
        ang = xi * tau
        pw_r = mag * jnp.cos(ang)
        pw_i = mag * jnp.sin(ang)
        a_r = pw_r[1:2]
        a_i = pw_i[1:2]
        den = lr * lr + li * li
        n_r = a_r - 1.0
        n_i = a_i
        co_r = (n_r * lr + n_i * li) / den
        co_i = (n_i * lr - n_r * li) / den
        bt_r = bre_ref[0, g]
        bt_i = bim_ref[0, g]
        bb_r = co_r * bt_r - co_i * bt_i
        bb_i = co_r * bt_i + co_i * bt_r
        c_r = cre_ref[0, g]
        c_i = cim_ref[0, g]
        ca = []
        for t in range(S5_CHUNK + 1):
            e_r = pw_r[t:t + 1]
            e_i = pw_i[t:t + 1]
            re = c_r * e_r - c_i * e_i
            im = c_r * e_i + c_i * e_r
            ca.append(jnp.concatenate([re, -im], axis=1))
        ct_ref[0, g] = jnp.concatenate(ca[1:], axis=0).T.astype(BF16)
        ca_all = jnp.concatenate(ca[:S5_CHUNK], axis=0)
        b2 = jnp.concatenate([bb_r, bb_i], axis=1)
        strip = _dot_nt(b2, ca_all, precision=lax.Precision.HIGHEST)
        for s in range(S5_CHUNK):
            if s == 0:
                blk = strip
            else:
                blk = jnp.where(lane >= 16 * s, pltpu.roll(strip, 16 * s, 1), 0.0)
            kb_ref[0, g, s * 16:(s + 1) * 16, 0:256] = blk.astype(BF16)
            e_r = pw_r[15 - s:16 - s]
            e_i = pw_i[15 - s:16 - s]
            bm_r = bb_r * e_r - bb_i * e_i
            bm_i = bb_r * e_i + bb_i * e_r
            kb_ref[0, g, s * 16:(s + 1) * 16, 256:512] = jnp.concatenate(
                [bm_r, bm_i, bm_i, bm_r], axis=1).astype(BF16)
        a16_r = pw_r[16:17]
        a16_i = pw_i[16:17]
        p_ref[0, g:g + 1, :] = jnp.concatenate([a16_r, a16_r], axis=1)
        q_ref[0, g:g + 1, :] = jnp.concatenate([-a16_i, a16_i], axis=1)


def _s5_prep(lam_re, lam_im, log_step, bt_re, bt_im, c_re, c_im):
    depth = lam_re.shape[0]
    gb = 8
    grid = (depth, N_GROUPS // gb)
    m3 = lambda l, g: (l, g, 0)
    m4 = lambda l, g: (l, g, 0, 0)
    return pl.pallas_call(
        functools.partial(_s5_prep_kernel, gb=gb),
        grid=grid,
        in_specs=[
            pl.BlockSpec((1, gb, P_STATE), m3),
            pl.BlockSpec((1, gb, P_STATE), m3),
            pl.BlockSpec((1, gb, 1), m3),
            pl.BlockSpec((1, gb, S5_GROUP, P_STATE), m4),
            pl.BlockSpec((1, gb, S5_GROUP, P_STATE), m4),
            pl.BlockSpec((1, gb, S5_GROUP, P_STATE), m4),
            pl.BlockSpec((1, gb, S5_GROUP, P_STATE), m4),
        ],
        out_specs=[
            pl.BlockSpec((1, gb, 256, 512), m4),
            pl.BlockSpec((1, gb, 128, 256), m4),
            pl.BlockSpec((1, gb, 128), m3),
            pl.BlockSpec((1, gb, 128), m3),
        ],
        out_shape=[
            jax.ShapeDtypeStruct((depth, N_GROUPS, 256, 512), BF16),
            jax.ShapeDtypeStruct((depth, N_GROUPS, 128, 256), BF16),
            jax.ShapeDtypeStruct((depth, N_GROUPS, 128), F32),
            jax.ShapeDtypeStruct((depth, N_GROUPS, 128), F32),
        ],
        compiler_params=_cparams(("arbitrary", "arbitrary"), 32),
        name="s5_prep",
    )(lam_re, lam_im, log_step.reshape(depth, N_GROUPS, 1), bt_re, bt_im, c_re, c_im)


def _block_transpose8(arrs, masks):
    arrs = list(arrs)
    for k in range(3):
        d = 1 << k
        sh = 16 * d
        new = list(arrs)
        for a in range(8):
            if a & d:
                continue
            b = a | d
            va, vb = arrs[a], arrs[b]
            new[a] = jnp.where(masks[k], pltpu.roll(vb, sh, 1), va)
            new[b] = jnp.where(masks[k], vb, pltpu.roll(va, LANES - sh, 1))
        arrs = new
    return arrs


def _lane_block_masks(rows):
    lane_blk = lax.broadcasted_iota(jnp.int32, (rows, LANES), 1) // 16
    return [((lane_blk >> k) & 1) == 1 for k in range(3)]


def _s5_kernel(u_ref, kb_ref, ct_ref, p_ref, q_ref, d_ref, h0_ref, *rest, nb, tl, gps, aliased):
    if aliased:
        rest = rest[2:]
    yf_ref, yb_ref, hout_ref, uflat, hb, hbs, hprev, ybuf, carry = rest
    i = pl.program_id(1)
    nt = pl.num_programs(1)
    nc = tl // S5_CHUNK
    cb = min(32, nc)
    nsub = nc // cb
    sr = cb * nb
    nh = gps // 8

    @pl.when(i == 0)
    def _():
        carry[...] = h0_ref[...]

    masks_f = _lane_block_masks(sr)
    masks_p = _lane_block_masks(sr // 2)

    def fwd(j, _):
        base = pl.multiple_of(j * (cb * S5_CHUNK * nb), 8)
        r0 = pl.multiple_of(j * sr, 16)
        for hh in range(nh):
            for sh in range(2):
                arrs = []
                for sl in range(8):
                    s = sh * 8 + sl
                    pieces = [u_ref[pl.ds(base + (cc * S5_CHUNK + s) * nb, nb),
                                    hh * LANES:(hh + 1) * LANES] for cc in range(cb)]
                    v = pieces[0] if cb == 1 else jnp.concatenate(pieces, axis=0)
                    arrs.append(pltpu.bitcast(v.astype(BF16), jnp.uint32))
                w = _block_transpose8(arrs, masks_p)
                for g in range(8):
                    uflat[hh * 8 + g, pl.ds(r0, sr), sh * LANES:(sh + 1) * LANES] = (
                        pltpu.bitcast(w[g], BF16))
        return 0

    lax.fori_loop(0, nsub, fwd, 0)

    for gi in range(gps):
        hbg = _dot(uflat[gi], kb_ref[gi, :, 256:512])
        hb[gi] = hbg[:, 0:LANES]
        hbs[gi] = hbg[:, LANES:2 * LANES]

    def step(c, hs):
        rows = pl.ds(pl.multiple_of(c * nb, 8), nb)
        out = []
        for gi in range(gps):
            h, hsw = hs[2 * gi], hs[2 * gi + 1]
            hprev[gi, rows, :] = h
            pg = p_ref[gi:gi + 1, :]
            qg = q_ref[gi:gi + 1, :]
            out.append(pg * h + qg * hsw + hb[gi, rows, :])
            out.append(pg * hsw - qg * h + hbs[gi, rows, :])
        return tuple(out)

    init = []
    for gi in range(gps):
        h = carry[gi]
        init += [h, pltpu.roll(h, P_STATE, 1)]
    hs = lax.fori_loop(0, nc, step, tuple(init))
    for gi in range(gps):
        carry[gi] = hs[2 * gi]

    for gi in range(gps):
        ybuf[gi] = (_dot(uflat[gi], kb_ref[gi, :, 0:256])
                    + _dot(hprev[gi].astype(BF16), ct_ref[gi]))

    def inv(j, _):
        base = pl.multiple_of(j * (cb * S5_CHUNK * nb), 8)
        r0 = pl.multiple_of(j * sr, 16)
        for hh in range(nh):
            lanes = slice(hh * LANES, (hh + 1) * LANES)
            dsk = d_ref[:, lanes]
            for sh in range(2):
                arrs = [ybuf[hh * 8 + g, pl.ds(r0, sr), sh * LANES:(sh + 1) * LANES]
                        for g in range(8)]
                z = _block_transpose8(arrs, masks_f)
                for sl in range(0, 8, 2):
                    t = sh * 8 + sl
                    for cc in range(cb):
                        rows = pl.ds(base + (cc * S5_CHUNK + t) * nb, 2 * nb)
                        y2 = jnp.concatenate([z[sl][cc * nb:(cc + 1) * nb, :],
                                              z[sl + 1][cc * nb:(cc + 1) * nb, :]], axis=0)
                        ya = _gelu(y2 + dsk * u_ref[rows, lanes])
                        yf_ref[rows, lanes] = ya
                        yb_ref[rows, lanes] = ya.astype(BF16)
        return 0

    lax.fori_loop(0, nsub, inv, 0)

    @pl.when(i == nt - 1)
    def _():
        hout_ref[...] = carry[...]


def _s5_branch(z, kb, ct, pp, qq, dskip, h0, prev, *, layer, nb, seq, tl, row0):
    gps = 16
    tm = tl * nb
    nt = seq // tl
    nblk = N_GROUPS // gps
    cw = gps * S5_GROUP
    r = (tl // S5_CHUNK) * nb
    rb0 = row0 // tm
    t_rows = z.shape[0]
    aliased = prev is not None
    in_specs = [
        pl.BlockSpec((tm, cw), lambda b, i: (rb0 + i, b)),
        pl.BlockSpec((None, gps, 256, 512), lambda b, i: (layer, b, 0, 0)),
        pl.BlockSpec((None, gps, 128, 256), lambda b, i: (layer, b, 0, 0)),
        pl.BlockSpec((None, gps, 128), lambda b, i: (layer, b, 0)),
        pl.BlockSpec((None, gps, 128), lambda b, i: (layer, b, 0)),
        pl.BlockSpec((None, 1, cw), lambda b, i: (layer, 0, b)),
        pl.BlockSpec((gps, nb, 128), lambda b, i: (b, 0, 0)),
    ]
    args = [z, kb, ct, pp, qq, dskip, h0]
    aliases = {}
    if aliased:
        in_specs += [_any_spec(), _any_spec()]
        args += list(prev)
        aliases = {7: 0, 8: 1}
    return pl.pallas_call(
        functools.partial(_s5_kernel, nb=nb, tl=tl, gps=gps, aliased=aliased),
        grid=(nblk, nt),
        in_specs=in_specs,
        out_specs=[
            pl.BlockSpec((tm, cw), lambda b, i: (rb0 + i, b)),
            pl.BlockSpec((tm, cw), lambda b, i: (rb0 + i, b)),
            pl.BlockSpec((gps, nb, 128), lambda b, i: (b, 0, 0)),
        ],
        out_shape=[
            jax.ShapeDtypeStruct((t_rows, D_MODEL), F32),
            jax.ShapeDtypeStruct((t_rows, D_MODEL), BF16),
            jax.ShapeDtypeStruct((N_GROUPS, nb, 128), F32),
        ],
        scratch_shapes=[
            pltpu.VMEM((gps, r, 256), BF16),
            pltpu.VMEM((gps, r, 128), F32),
            pltpu.VMEM((gps, r, 128), F32),
            pltpu.VMEM((gps, r, 128), F32),
            pltpu.VMEM((gps, r, 256), F32),
            pltpu.VMEM((gps, nb, 128), F32),
        ],
        input_output_aliases=aliases,
        compiler_params=_cparams(("arbitrary", "arbitrary"), 58),
        name="s5_branch",
    )(*args)


def _lru_kernel(xb_ref, gt_ref, cw_ref, cb_ref, wri_ref, br_ref, bi_ref, lam_ref,
                h0_ref, cbuf_ref, *rest, nb, tl, rc, nblk, aliased):
    if aliased:
        rest = rest[1:]
    y_ref, hout_ref, cout_ref, hist_s, a_s, b_s, hc = rest
    i = pl.program_id(1)
    nt = pl.num_programs(1)
    tm = tl * nb
    hist = (CONV_W - 1) * nb

    @pl.when(i == 0)
    def _():
        hc[...] = h0_ref[...]
        hist_s[...] = cbuf_ref[...]

    cw = cw_ref[...]
    cbias = cb_ref[...]
    br = br_ref[...]
    bi = bi_ref[...]
    nlam = -lam_ref[...]
    sp = jnp.maximum(nlam, 0.0) + jnp.log1p(jnp.exp(-jnp.abs(nlam)))
    c1 = (-0.5 * LRU_C) * sp

    def gate_math(taps, rows):
        for j in range(nblk):
            ln = slice(j * BLOCK_B, (j + 1) * BLOCK_B)
            acc = cw[0:1, ln] * taps[0][:, ln]
            for k in range(1, CONV_W):
                acc = acc + cw[k:k + 1, ln] * taps[k][:, ln]
            xc = cbias[:, ln] + acc
            gm = _dot(xc.astype(BF16), wri_ref[j])
            th_r = jnp.tanh(0.5 * (gm[:, 0:BLOCK_B] + br[:, ln]))
            th_i = jnp.tanh(0.5 * (gm[:, BLOCK_B:2 * BLOCK_B] + bi[:, ln]))
            log_a = c1[:, ln] * th_r + c1[:, ln]
            a = jnp.exp(log_a)
            mag = _sqrt_nonneg(-jnp.tanh(log_a) * (a * a + 1.0))
            a_s[rows, ln] = a
            b_s[rows, ln] = mag * ((0.5 * th_i + 0.5) * xc)

    head = jnp.concatenate([hist_s[...], xb_ref[0:rc, :]], axis=0)
    gate_math([head[k * nb:k * nb + rc, :] for k in range(CONV_W)], pl.ds(0, rc))

    def gates(c, _):
        r0 = pl.multiple_of(c * rc, rc)
        taps = [xb_ref[pl.ds(r0 - (CONV_W - 1 - k) * nb, rc), :] for k in range(CONV_W)]
        gate_math(taps, pl.ds(r0, rc))
        return 0

    lax.fori_loop(1, tm // rc, gates, 0)
    hist_s[...] = xb_ref[tm - hist:tm, :]

    def scan(t, h):
        rows = pl.ds(pl.multiple_of(t * nb, 8), nb)
        h = a_s[rows, :] * h + b_s[rows, :]
        b_s[rows, :] = h
        return h

    h = lax.fori_loop(0, tl, scan, hc[...], unroll=8)
    hc[...] = h

    def outp(c, _):
        rows = pl.ds(pl.multiple_of(c * rc, rc), rc)
        y_ref[rows, :] = (b_s[rows, :] * _gelu(gt_ref[rows, :])).astype(BF16)
        return 0

    lax.fori_loop(0, tm // rc, outp, 0)

    @pl.when(i == nt - 1)
    def _():
        hout_ref[...] = h
        cout_ref[...] = hist_s[...]


def _lru_branch(z, conv_w, conv_b, wri, b_r, b_i, lam, h0, cbuf, prev, *, layer, nb, seq, tl,
                row0):
    tm = tl * nb
    nt = seq // tl
    rc = min(256, tm)
    hist = (CONV_W - 1) * nb
    nblk = min(4, N_BLOCKS_B)
    lw = nblk * BLOCK_B
    xoff = D_MODEL // lw
    rb0 = row0 // tm
    aliased = prev is not None
    col = lambda k, i: (0, k)
    lcol = lambda k, i: (layer, 0, k)
    in_specs = [
        pl.BlockSpec((tm, lw), lambda k, i: (rb0 + i, xoff + k)),
        pl.BlockSpec((tm, lw), lambda k, i: (rb0 + i, 2 * xoff + k)),
        pl.BlockSpec((None, CONV_W, lw), lcol),
        pl.BlockSpec((None, 1, lw), lcol),
        pl.BlockSpec((None, nblk, BLOCK_B, 2 * BLOCK_B), lambda k, i: (layer, k, 0, 0)),
        pl.BlockSpec((None, 1, lw), lcol),
        pl.BlockSpec((None, 1, lw), lcol),
        pl.BlockSpec((None, 1, lw), lcol),
        pl.BlockSpec((nb, lw), col),
        pl.BlockSpec((hist, lw), col),
    ]
    args = [z, z, conv_w, conv_b, wri, b_r, b_i, lam, h0, cbuf]
    aliases = {}
    if aliased:
        in_specs.append(_any_spec())
        args.append(prev)
        aliases = {10: 0}
    return pl.pallas_call(
        functools.partial(_lru_kernel, nb=nb, tl=tl, rc=rc, nblk=nblk, aliased=aliased),
        grid=(N_BLOCKS_B // nblk, nt),
        in_specs=in_specs,
        out_specs=[
            pl.BlockSpec((tm, lw), lambda k, i: (rb0 + i, k)),
            pl.BlockSpec((nb, lw), col),
            pl.BlockSpec((hist, lw), col),
        ],
        out_shape=[
            jax.ShapeDtypeStruct((z.shape[0], D_MODEL), BF16),
            jax.ShapeDtypeStruct((nb, D_MODEL), F32),
            jax.ShapeDtypeStruct((hist, D_MODEL), F32),
        ],
        scratch_shapes=[
            pltpu.VMEM((hist, lw), F32),
            pltpu.VMEM((tm, lw), F32),
            pltpu.VMEM((tm, lw), F32),
            pltpu.VMEM((nb, lw), F32),
        ],
        input_output_aliases=aliases,
        compiler_params=_cparams(("arbitrary", "arbitrary"), 48),
        name="lru_branch",
    )(*args)


ROW_TILE = 256


def _rows_scratch():
    return pltpu.VMEM((D_MODEL // LANES, ROW_TILE, LANES), F32)


def _rows_from_batch_major(scr, x_ref, nb):
    steps = x_ref.shape[1]
    for n in range(nb):
        for c in range(D_MODEL // LANES):
            scr[c, pl.ds(n, steps, stride=nb), :] = x_ref[n, :, c * LANES:(c + 1) * LANES]
    return jnp.concatenate([scr[c] for c in range(D_MODEL // LANES)], axis=1)


def _rows_to_batch_major(scr, x, o_ref, nb):
    steps = o_ref.shape[1]
    for c in range(D_MODEL // LANES):
        scr[c] = x[:, c * LANES:(c + 1) * LANES]
    for n in range(nb):
        for c in range(D_MODEL // LANES):
            o_ref[n, :, c * LANES:(c + 1) * LANES] = scr[c, pl.ds(n, steps, stride=nb), :]


def _rmsnorm_rows(x, g_ref):
    inv = lax.rsqrt(jnp.mean(x * x, axis=-1, keepdims=True) + EPS)
    return x * inv * g_ref[...]


def _prenorm_kernel(x_ref, g_ref, *rest, nb, aliased):
    if aliased:
        rest = rest[1:]
    o_ref, scr = rest
    x = _rows_from_batch_major(scr, x_ref, nb)
    o_ref[...] = _rmsnorm_rows(x, g_ref).astype(o_ref.dtype)


def _prenorm(x_bm, g, prev, *, layer, row0, total_rows):
    nb, seq, _ = x_bm.shape
    steps = ROW_TILE // nb
    rb0 = row0 // ROW_TILE
    in_specs = [pl.BlockSpec((nb, steps, D_MODEL), lambda i: (0, i, 0)),
                pl.BlockSpec((None, 1, D_MODEL), lambda i: (layer, 0, 0))]
    args = [x_bm, g]
    aliases = {}
    if prev is not None:
        in_specs.append(_any_spec())
        args.append(prev)
        aliases = {2: 0}
    return pl.pallas_call(
        functools.partial(_prenorm_kernel, nb=nb, aliased=prev is not None),
        grid=(seq // steps,),
        in_specs=in_specs,
        out_specs=pl.BlockSpec((ROW_TILE, D_MODEL), lambda i: (rb0 + i, 0)),
        out_shape=jax.ShapeDtypeStruct((total_rows, D_MODEL), BF16),
        scratch_shapes=[_rows_scratch()],
        input_output_aliases=aliases,
        compiler_params=_cparams(("arbitrary",), 40),
        name="prenorm",
    )(*args)


def _finish_kernel(x_ref, f_ref, g_ref, *rest, nb, x_bm, out_bm, with_h, n_alias):
    rest = list(rest)
    gn_ref = rest.pop(0) if with_h else None
    rest = rest[n_alias:]
    xo_ref = rest.pop(0)
    h_ref = rest.pop(0) if with_h else None
    x = _rows_from_batch_major(rest[0], x_ref, nb) if x_bm else x_ref[...]
    x = x + _rmsnorm_rows(f_ref[...], g_ref)
    if out_bm:
        _rows_to_batch_major(rest[0], x, xo_ref, nb)
    else:
        xo_ref[...] = x
    if with_h:
        h_ref[...] = _rmsnorm_rows(x, gn_ref).astype(h_ref.dtype)


def _finish(x, f, g, layer, g_next, layer_next, prev, *, nb, seq, row0, x_bm, out_bm):
    total_rows = f.shape[0]
    steps = ROW_TILE // nb
    rb0 = row0 // ROW_TILE
    with_h = g_next is not None
    row = pl.BlockSpec((ROW_TILE, D_MODEL), lambda i: (rb0 + i, 0))
    bm = pl.BlockSpec((nb, steps, D_MODEL), lambda i: (0, i, 0))
    vec = lambda l: pl.BlockSpec((None, 1, D_MODEL), lambda i: (l, 0, 0))
    in_specs = [bm if x_bm else row, row, vec(layer)]
    args = [x, f, g]
    if with_h:
        in_specs.append(vec(layer_next))
        args.append(g_next)
    aliases = {}
    if prev is not None:
        for k, p in enumerate(prev):
            aliases[len(args)] = k
            in_specs.append(_any_spec())
            args.append(p)
    out_specs = [bm if out_bm else row]
    out_shape = [jax.ShapeDtypeStruct((nb, seq, D_MODEL) if out_bm else (total_rows, D_MODEL), F32)]
    if with_h:
        out_specs.append(row)
        out_shape.append(jax.ShapeDtypeStruct((total_rows, D_MODEL), BF16))
    scratch = [_rows_scratch()] if (x_bm or out_bm) else []
    outs = pl.pallas_call(
        functools.partial(_finish_kernel, nb=nb, x_bm=x_bm, out_bm=out_bm, with_h=with_h,
                          n_alias=len(aliases)),
        grid=(seq * nb // ROW_TILE,),
        in_specs=in_specs,
        out_specs=out_specs,
        out_shape=out_shape,
        scratch_shapes=scratch,
        input_output_aliases=aliases,
        compiler_params=_cparams(("arbitrary",), 48),
        name="finish",
    )(*args)
    return (outs[0], outs[1]) if with_h else (outs[0], None)


def _mm_kernel(a_ref, w_ref, o_ref):
    o_ref[...] = _dot(a_ref[...], w_ref[...]).astype(o_ref.dtype)


def _matmul(a, w, *, layer, tm, tn, vmem_mib=48):
    t, k = a.shape
    n = w.shape[2]
    return pl.pallas_call(
        _mm_kernel,
        grid=(t // tm, n // tn),
        in_specs=[pl.BlockSpec((tm, k), lambda i, j: (i, 0)),
                  pl.BlockSpec((None, k, tn), lambda i, j: (layer, 0, j))],
        out_specs=pl.BlockSpec((tm, tn), lambda i, j: (i, j)),
        out_shape=jax.ShapeDtypeStruct((t, n), F32),
        compiler_params=_cparams(("arbitrary", "arbitrary"), vmem_mib),
        name="matmul",
    )(a, w)


def _glu_kernel(a_ref, w_ref, b_ref, y_ref, o_ref):
    gate = _sigmoid(_dot(a_ref[...], w_ref[...]) + b_ref[...])
    o_ref[...] = (y_ref[...] * gate).astype(o_ref.dtype)


def _glu(ya_b, ya_f, w, b, *, layer, tm, tn):
    t = ya_b.shape[0]
    return pl.pallas_call(
        _glu_kernel,
        grid=(t // tm, D_MODEL // tn),
        in_specs=[pl.BlockSpec((tm, D_MODEL), lambda i, j: (i, 0)),
                  pl.BlockSpec((None, D_MODEL, tn), lambda i, j: (layer, 0, j)),
                  pl.BlockSpec((None, 1, tn), lambda i, j: (layer, 0, j)),
                  pl.BlockSpec((tm, tn), lambda i, j: (i, j))],
        out_specs=pl.BlockSpec((tm, tn), lambda i, j: (i, j)),
        out_shape=jax.ShapeDtypeStruct((t, D_MODEL), BF16),
        compiler_params=_cparams(("arbitrary", "arbitrary"), 48),
        name="glu",
    )(ya_b, w, b, ya_f)


def _merge_kernel(a1_ref, w1_ref, a2_ref, w2_ref, ga_ref, gb_ref, o_ref):
    m = (_sigmoid(ga_ref[...]) * _dot(a1_ref[...], w1_ref[...])
         + _sigmoid(gb_ref[...]) * _dot(a2_ref[...], w2_ref[...]))
    o_ref[...] = m.astype(o_ref.dtype)


def _merge(ya, p_a, yb, p_b, z, *, layer, tm, tn):
    t = ya.shape[0]
    ga_off = 3 * D_MODEL // tn
    gb_off = 4 * D_MODEL // tn
    wspec = pl.BlockSpec((None, D_MODEL, tn), lambda i, j: (layer, 0, j))
    return pl.pallas_call(
        _merge_kernel,
        grid=(t // tm, D_MODEL // tn),
        in_specs=[pl.BlockSpec((tm, D_MODEL), lambda i, j: (i, 0)),
                  wspec,
                  pl.BlockSpec((tm, D_MODEL), lambda i, j: (i, 0)),
                  wspec,
                  pl.BlockSpec((tm, tn), lambda i, j: (i, ga_off + j)),
                  pl.BlockSpec((tm, tn), lambda i, j: (i, gb_off + j))],
        out_specs=pl.BlockSpec((tm, tn), lambda i, j: (i, j)),
        out_shape=jax.ShapeDtypeStruct((t, D_MODEL), BF16),
        compiler_params=_cparams(("arbitrary", "arbitrary"), 56),
        name="merge",
    )(ya, p_a, yb, p_b, z, z)


def _ffn_up_kernel(a_ref, wg_ref, wu_ref, o_ref):
    a = a_ref[...]
    g = _dot(a, wg_ref[...])
    u = _dot(a, wu_ref[...])
    o_ref[...] = (g * _sigmoid(g) * u).astype(o_ref.dtype)


def _ffn_up(h, wg, wu, *, layer, tm, tn):
    t = h.shape[0]
    wspec = pl.BlockSpec((None, D_MODEL, tn), lambda i, j: (layer, 0, j))
    return pl.pallas_call(
        _ffn_up_kernel,
        grid=(t // tm, D_FF // tn),
        in_specs=[pl.BlockSpec((tm, D_MODEL), lambda i, j: (i, 0)), wspec, wspec],
        out_specs=pl.BlockSpec((tm, tn), lambda i, j: (i, j)),
        out_shape=jax.ShapeDtypeStruct((t, D_FF), BF16),
        compiler_params=_cparams(("arbitrary", "arbitrary"), 48),
        name="ffn_up",
    )(h, wg, wu)


def kernel(x_prompt, x_sample, state_s5_re, state_s5_im, state_lru, cache_conv, g_pre_mix, w_in, s5_lam_re, s5_lam_im, s5_log_step, s5_b_re, s5_b_im, s5_c_re, s5_c_im, s5_d, w_glu, b_glu, conv_w, conv_b, lru_w_r, lru_b_r, lru_w_i, lru_b_i, lru_lam, p_a, p_b, w_out, g_post_mix, g_pre_ffn, w_gate, w_up, w_down, g_post_ffn):
    depth = w_in.shape[0]
    nbp, lp, _ = x_prompt.shape
    nbs, ls, _ = x_sample.shape
    sd = state_lru.dtype
    rows_p = nbp * lp
    rows_s = nbs * ls
    hist_p = (CONV_W - 1) * nbp
    hist_s = (CONV_W - 1) * nbs
    rows = rows_p + rows_s
    tm = 1280 if rows % 1280 == 0 else 256
    tm_half = tm // 2
    tl_s5 = min(512, lp)
    tl_lru = min(128, lp)
    seg_p = dict(nb=nbp, seq=lp, row0=0)
    seg_s = dict(nb=nbs, seq=ls, row0=rows_p)

    kb, ct, pp, qq = _s5_prep(s5_lam_re, s5_lam_im, s5_log_step,
                              s5_b_re.transpose(0, 1, 3, 2), s5_b_im.transpose(0, 1, 3, 2),
                              s5_c_re, s5_c_im)
    vec = lambda a: a.reshape(depth, 1, -1)
    w_in_b = w_in.astype(BF16)
    w_glu_b = w_glu.astype(BF16)
    p_a_b = p_a.astype(BF16)
    p_b_b = p_b.astype(BF16)
    w_out_b = w_out.astype(BF16)
    w_gate_b = w_gate.astype(BF16)
    w_up_b = w_up.astype(BF16)
    w_down_b = w_down.astype(BF16)
    wri = jnp.concatenate([lru_w_r, lru_w_i], axis=-1).astype(BF16)
    g_pre_mix, g_post_mix, g_pre_ffn, g_post_ffn = map(
        vec, (g_pre_mix, g_post_mix, g_pre_ffn, g_post_ffn))
    s5_d, b_glu, conv_b, lru_b_r, lru_b_i, lru_lam = map(
        vec, (s5_d, b_glu, conv_b, lru_b_r, lru_b_i, lru_lam))

    h = _prenorm(x_prompt, g_pre_mix, None, layer=0, row0=0, total_rows=rows)
    h = _prenorm(x_sample, g_pre_mix, h, layer=0, row0=rows_p, total_rows=rows)
    zeros_s5 = jnp.zeros((N_GROUPS, nbp, 2 * P_STATE), F32)
    zeros_lru = jnp.zeros((nbp, D_MODEL), F32)
    zeros_conv = jnp.zeros((hist_p, D_MODEL), F32)
    outs_p, outs_s = [], []
    for l in range(depth):
        z = _matmul(h, w_in_b, layer=l, tm=tm, tn=512)

        ya_f, ya_b, s5p = _s5_branch(z, kb, ct, pp, qq, s5_d, zeros_s5, None,
                                     layer=l, nb=nbp, seq=lp, tl=tl_s5, row0=0)
        s5_h0 =jnp.concatenate([state_s5_re[l], state_s5_im[l]], axis=-1).transpose(1, 0, 2)
        ya_f, ya_b, s5s = _s5_branch(z, kb, ct, pp, qq, s5_d, s5_h0.astype(F32), (ya_f, ya_b),
                                     layer=l, nb=nbs, seq=ls, tl=ls, row0=rows_p)
        lru_args = (conv_w, conv_b, wri, lru_b_r, lru_b_i, lru_lam)
        yb, lrup, convp = _lru_branch(z, *lru_args, zeros_lru, zeros_conv, None,
                                      layer=l, nb=nbp, seq=lp, tl=tl_lru, row0=0)
        cbuf = cache_conv[l].transpose(1, 0, 2).reshape(hist_s, D_MODEL)
        yb, lrus, convs = _lru_branch(z, *lru_args, state_lru[l].astype(F32), cbuf.astype(F32),
                                      yb, layer=l, nb=nbs, seq=ls, tl=ls, row0=rows_p)
        outs_p.append((s5p, lrup, convp))
        outs_s.append((s5s, lrus, convs))

        ya2 = _glu(ya_b, ya_f, w_glu_b, b_glu, layer=l, tm=tm, tn=512)
        merged = _merge(ya2, p_a_b, yb, p_b_b, z, layer=l, tm=tm_half, tn=512)
        f = _matmul(merged, w_out_b, layer=l, tm=tm, tn=512)
        if l == 0:
            x, h = _finish(x_prompt, f, g_post_mix, l, g_pre_ffn, l, None,
                           x_bm=True, out_bm=False, **seg_p)
            x, h = _finish(x_sample, f, g_post_mix, l, g_pre_ffn, l, (x, h),
                           x_bm=True, out_bm=False, **seg_s)
        else:
            x, h = _finish(x, f, g_post_mix, l, g_pre_ffn, l, None,
                           nb=1, seq=rows, row0=0, x_bm=False, out_bm=False)
        hmid = _ffn_up(h, w_gate_b, w_up_b, layer=l, tm=tm, tn=256)
        f = _matmul(hmid, w_down_b, layer=l, tm=tm_half, tn=256)
        if l + 1 < depth:
            x, h = _finish(x, f, g_post_ffn, l, g_pre_mix, l + 1, None,
                           nb=1, seq=rows, row0=0, x_bm=False, out_bm=False)
        else:
            y_prompt, _ = _finish(x, f, g_post_ffn, l, None, None, None,
                                  x_bm=False, out_bm=True, **seg_p)
            y_sample, _ = _finish(x, f, g_post_ffn, l, None, None, None,
                                  x_bm=False, out_bm=True, **seg_s)

    def pack(outs, nb):
        s5 = jnp.stack([o[0] for o in outs]).transpose(0, 2, 1, 3)
        lru = jnp.stack([o[1] for o in outs])
        conv = jnp.stack([o[2] for o in outs]).reshape(depth, CONV_W - 1, nb, D_MODEL)
        return (s5[..., :P_STATE].astype(sd), s5[..., P_STATE:].astype(sd), lru.astype(sd),
                conv.transpose(0, 2, 1, 3).astype(sd))

    p_re, p_im, p_lru, p_conv = pack(outs_p, nbp)
    s_re, s_im, s_lru, s_conv = pack(outs_s, nbs)
    return (y_prompt, y_sample, p_re, p_im, p_lru, p_conv, s_re, s_im, s_lru, s_conv)
```

```python
import functools
import math

import jax
import jax.numpy as jnp
from jax import lax
from jax.experimental import pallas as pl
from jax.experimental.pallas import tpu as pltpu

F32 = jnp.float32
BF16 = jnp.bfloat16

D_MODEL = 4096
S5_GROUP = 16
N_GROUPS = D_MODEL // S5_GROUP
P_STATE = 64
S5_CHUNK = 16
N_BLOCKS_B = 16
BLOCK_B = D_MODEL // N_BLOCKS_B
CONV_W = 4
LRU_C = 8.0
D_FF = 11008
EPS = 1e-6
LANES = 128
MIB = 1024 * 1024


def _cparams(sem, vmem_mib):
    return pltpu.CompilerParams(dimension_semantics=sem, vmem_limit_bytes=vmem_mib * MIB)


def _sigmoid(x):
    return 0.5 * (jnp.tanh(0.5 * x) + 1.0)


def _gelu(x):
    c = math.sqrt(2.0 / math.pi)
    hx = 0.5 * x
    return hx + hx * jnp.tanh(x * (c + (c * 0.044715) * (x * x)))


def _sqrt_nonneg(v):
    return jnp.where(v == 0.0, 0.0, v * lax.rsqrt(v))


def _dot(a, b):
    return jnp.dot(a, b, preferred_element_type=F32)


def _dot_nt(a, b, precision=None):
    return lax.dot_general(a, b, (((1,), (1,)), ((), ())), precision=precision,
                           preferred_element_type=F32)


def _any_spec():
    return pl.BlockSpec(memory_space=pl.ANY)


def _s5_prep_kernel(lre_ref, lim_ref, ls_ref, bre_ref, bim_ref, cre_ref, cim_ref,
                    kb_ref, ct_ref, p_ref, q_ref, *, gb):
    tau = lax.broadcasted_iota(jnp.int32, (24, P_STATE), 0).astype(F32)
    lane = lax.broadcasted_iota(jnp.int32, (S5_GROUP, 2 * LANES), 1)
    for g in range(gb):
        lr = lre_ref[0, g:g + 1, :]
        li = lim_ref[0, g:g + 1, :]
        step = jnp.exp(ls_ref[0, g:g + 1, :])
        xr = lr * step
        xi = li * step
        mag = jnp.exp(xr * tau)
        ang = xi * tau
        pw_r = mag * jnp.cos(ang)
        pw_i = mag * jnp.sin(ang)
        a_r = pw_r[1:2]
        a_i = pw_i[1:2]
        den = lr * lr + li * li
        n_r = a_r - 1.0
        n_i = a_i
        co_r = (n_r * lr + n_i * li) / den
        co_i = (n_i * lr - n_r * li) / den
        bt_r = bre_ref[0, g]
        bt_i = bim_ref[0, g]
        bb_r = co_r * bt_r - co_i * bt_i
        bb_i = co_r * bt_i + co_i * bt_r
        c_r = cre_ref[0, g]
        c_i = cim_ref[0, g]
        ca = []
        for t in range(S5_CHUNK + 1):
            e_r = pw_r[t:t + 1]
            e_i = pw_i[t:t + 1]
            re = c_r * e_r - c_i * e_i
            im = c_r * e_i + c_i * e_r
            ca.append(jnp.concatenate([re, -im], axis=1))
        ct_ref[0, g] = jnp.concatenate(ca[1:], axis=0).T.astype(BF16)
        ca_all = jnp.concatenate(ca[:S5_CHUNK], axis=0)
        b2 = jnp.concatenate([bb_r, bb_i], axis=1)
        strip = _dot_nt(b2, ca_all, precision=lax.Precision.HIGHEST)
        for s in range(S5_CHUNK):
            if s == 0:
                blk = strip
            else:
                blk = jnp.where(lane >= 16 * s, pltpu.roll(strip, 16 * s, 1), 0.0)
            kb_ref[0, g, s * 16:(s + 1) * 16, 0:256] = blk.astype(BF16)
            e_r = pw_r[15 - s:16 - s]
            e_i = pw_i[15 - s:16 - s]
            bm_r = bb_r * e_r - bb_i * e_i
            bm_i = bb_r * e_i + bb_i * e_r
            kb_ref[0, g, s * 16:(s + 1) * 16, 256:512] = jnp.concatenate(
                [bm_r, bm_i, bm_i, bm_r], axis=1).astype(BF16)
        a16_r = pw_r[16:17]
        a16_i = pw_i[16:17]
        p_ref[0, g:g + 1, :] = jnp.concatenate([a16_r, a16_r], axis=1)
        q_ref[0, g:g + 1, :] = jnp.concatenate([-a16_i, a16_i], axis=1)


def _s5_prep(lam_re, lam_im, log_step, bt_re, bt_im, c_re, c_im):
    depth = lam_re.shape[0]
    gb = 8
    grid = (depth, N_GROUPS // gb)
    m3 = lambda l, g: (l, g, 0)
    m4 = lambda l, g: (l, g, 0, 0)
    return pl.pallas_call(
        functools.partial(_s5_prep_kernel, gb=gb),
        grid=grid,
        in_specs=[
            pl.BlockSpec((1, gb, P_STATE), m3),
            pl.BlockSpec((1, gb, P_STATE), m3),
            pl.BlockSpec((1, gb, 1), m3),
            pl.BlockSpec((1, gb, S5_GROUP, P_STATE), m4),
            pl.BlockSpec((1, gb, S5_GROUP, P_STATE), m4),
            pl.BlockSpec((1, gb, S5_GROUP, P_STATE), m4),
            pl.BlockSpec((1, gb, S5_GROUP, P_STATE), m4),
        ],
        out_specs=[
            pl.BlockSpec((1, gb, 256, 512), m4),
            pl.BlockSpec((1, gb, 128, 256), m4),
            pl.BlockSpec((1, gb, 128), m3),
            pl.BlockSpec((1, gb, 128), m3),
        ],
        out_shape=[
            jax.ShapeDtypeStruct((depth, N_GROUPS, 256, 512), BF16),
            jax.ShapeDtypeStruct((depth, N_GROUPS, 128, 256), BF16),
            jax.ShapeDtypeStruct((depth, N_GROUPS, 128), F32),
            jax.ShapeDtypeStruct((depth, N_GROUPS, 128), F32),
        ],
        compiler_params=_cparams(("arbitrary", "arbitrary"), 32),
        name="s5_prep",
    )(lam_re, lam_im, log_step.reshape(depth, N_GROUPS, 1), bt_re, bt_im, c_re, c_im)


def _block_transpose8(arrs, masks):
    arrs = list(arrs)
    for k in range(3):
        d = 1 << k
        sh = 16 * d
        new = list(arrs)
        for a in range(8):
            if a & d:
                continue
            b = a | d
            va, vb = arrs[a], arrs[b]
            new[a] = jnp.where(masks[k], pltpu.roll(vb, sh, 1), va)
            new[b] = jnp.where(masks[k], vb, pltpu.roll(va, LANES - sh, 1))
        arrs = new
    return arrs


def _lane_block_masks(rows):
    lane_blk = lax.broadcasted_iota(jnp.int32, (rows, LANES), 1) // 16
    return [((lane_blk >> k) & 1) == 1 for k in range(3)]


def _s5_kernel(u_ref, kb_ref, ct_ref, p_ref, q_ref, d_ref, h0_ref, *rest, nb, tl, gps, aliased):
    if aliased:
        rest = rest[2:]
    yf_ref, yb_ref, hout_ref, uflat, hb, hbs, hprev, ybuf, carry = rest
    i = pl.program_id(1)
    nt = pl.num_programs(1)
    nc = tl // S5_CHUNK
    cb = min(32, nc)
    nsub = nc // cb
    sr = cb * nb
    nh = gps // 8

    @pl.when(i == 0)
    def _():
        carry[...] = h0_ref[...]

    masks_f = _lane_block_masks(sr)
    masks_p = _lane_block_masks(sr // 2)

    def fwd(j, _):
        base = pl.multiple_of(j * (cb * S5_CHUNK * nb), 8)
        r0 = pl.multiple_of(j * sr, 16)
        for hh in range(nh):
            for sh in range(2):
                arrs = []
                for sl in range(8):
                    s = sh * 8 + sl
                    pieces = [u_ref[pl.ds(base + (cc * S5_CHUNK + s) * nb, nb),
                                    hh * LANES:(hh + 1) * LANES] for cc in range(cb)]
                    v = pieces[0] if cb == 1 else jnp.concatenate(pieces, axis=0)
                    arrs.append(pltpu.bitcast(v.astype(BF16), jnp.uint32))
                w = _block_transpose8(arrs, masks_p)
                for g in range(8):
                    uflat[hh * 8 + g, pl.ds(r0, sr), sh * LANES:(sh + 1) * LANES] = (
                        pltpu.bitcast(w[g], BF16))
        return 0

    lax.fori_loop(0, nsub, fwd, 0)

    for gi in range(gps):
        hbg = _dot(uflat[gi], kb_ref[gi, :, 256:512])
        hb[gi] = hbg[:, 0:LANES]
        hbs[gi] = hbg[:, LANES:2 * LANES]

    def step(c, hs):
        rows = pl.ds(pl.multiple_of(c * nb, 8), nb)
        out = []
        for gi in range(gps):
            h, hsw = hs[2 * gi], hs[2 * gi + 1]
            hprev[gi, rows, :] = h
            pg = p_ref[gi:gi + 1, :]
            qg = q_ref[gi:gi + 1, :]
            out.append(pg * h + qg * hsw + hb[gi, rows, :])
            out.append(pg * hsw - qg * h + hbs[gi, rows, :])
        return tuple(out)

    init = []
    for gi in range(gps):
        h = carry[gi]
        init += [h, pltpu.roll(h, P_STATE, 1)]
    hs = lax.fori_loop(0, nc, step, tuple(init))
    for gi in range(gps):
        carry[gi] = hs[2 * gi]

    for gi in range(gps):
        ybuf[gi] = (_dot(uflat[gi], kb_ref[gi, :, 0:256])
                    + _dot(hprev[gi].astype(BF16), ct_ref[gi]))

    def inv(j, _):
        base = pl.multiple_of(j * (cb * S5_CHUNK * nb), 8)
        r0 = pl.multiple_of(j * sr, 16)
        for hh in range(nh):
            lanes = slice(hh * LANES, (hh + 1) * LANES)
            dsk = d_ref[:, lanes]
            for sh in range(2):
                arrs = [ybuf[hh * 8 + g, pl.ds(r0, sr), sh * LANES:(sh + 1) * LANES]
                        for g in range(8)]
                z = _block_transpose8(arrs, masks_f)
                for sl in range(0, 8, 2):
                    t = sh * 8 + sl
                    for cc in range(cb):
                        rows = pl.ds(base + (cc * S5_CHUNK + t) * nb, 2 * nb)
                        y2 = jnp.concatenate([z[sl][cc * nb:(cc + 1) * nb, :],
                                              z[sl + 1][cc * nb:(cc + 1) * nb, :]], axis=0)
                        ya = _gelu(y2 + dsk * u_ref[rows, lanes])
                        yf_ref[rows, lanes] = ya
                        yb_ref[rows, lanes] = ya.astype(BF16)
        return 0

    lax.fori_loop(0, nsub, inv, 0)

    @pl.when(i == nt - 1)
    def _():
        hout_ref[...] = carry[...]


def _s5_branch(z, kb, ct, pp, qq, dskip, h0, prev, *, layer, nb, seq, tl, row0):
    gps = 16
    tm = tl * nb
    nt = seq // tl
    nblk = N_GROUPS // gps
    cw = gps * S5_GROUP
    r = (tl // S5_CHUNK) * nb
    rb0 = row0 // tm
    t_rows = z.shape[0]
    aliased = prev is not None
    in_specs = [
        pl.BlockSpec((tm, cw), lambda b, i: (rb0 + i, b)),
        pl.BlockSpec((None, gps, 256, 512), lambda b, i: (layer, b, 0, 0)),
        pl.BlockSpec((None, gps, 128, 256), lambda b, i: (layer, b, 0, 0)),
        pl.BlockSpec((None, gps, 128), lambda b, i: (layer, b, 0)),
        pl.BlockSpec((None, gps, 128), lambda b, i: (layer, b, 0)),
        pl.BlockSpec((None, 1, cw), lambda b, i: (layer, 0, b)),
        pl.BlockSpec((gps, nb, 128), lambda b, i: (b, 0, 0)),
    ]
    args = [z, kb, ct, pp, qq, dskip, h0]
    aliases = {}
    if aliased:
        in_specs += [_any_spec(), _any_spec()]
        args += list(prev)
        aliases = {7: 0, 8: 1}
    return pl.pallas_call(
        functools.partial(_s5_kernel, nb=nb, tl=tl, gps=gps, aliased=aliased),
        grid=(nblk, nt),
        in_specs=in_specs,
        out_specs=[
            pl.BlockSpec((tm, cw), lambda b, i: (rb0 + i, b)),
            pl.BlockSpec((tm, cw), lambda b, i: (rb0 + i, b)),
            pl.BlockSpec((gps, nb, 128), lambda b, i: (b, 0, 0)),
        ],
        out_shape=[
            jax.ShapeDtypeStruct((t_rows, D_MODEL), F32),
            jax.ShapeDtypeStruct((t_rows, D_MODEL), BF16),
            jax.ShapeDtypeStruct((N_GROUPS, nb, 128), F32),
        ],
        scratch_shapes=[
            pltpu.VMEM((gps, r, 256), BF16),
            pltpu.VMEM((gps, r, 128), F32),
            pltpu.VMEM((gps, r, 128), F32),
            pltpu.VMEM((gps, r, 128), F32),
            pltpu.VMEM((gps, r, 256), F32),
            pltpu.VMEM((gps, nb, 128), F32),
        ],
        input_output_aliases=aliases,
        compiler_params=_cparams(("arbitrary", "arbitrary"), 58),
        name="s5_branch",
    )(*args)


def _lru_kernel(xb_ref, gt_ref, cw_ref, cb_ref, wri_ref, br_ref, bi_ref, lam_ref,
                h0_ref, cbuf_ref, *rest, nb, tl, rc, nblk, aliased):
    if aliased:
        rest = rest[1:]
    y_ref, hout_ref, cout_ref, hist_s, a_s, b_s, hc = rest
    i = pl.program_id(1)
    nt = pl.num_programs(1)
    tm = tl * nb
    hist = (CONV_W - 1) * nb

    @pl.when(i == 0)
    def _():
        hc[...] = h0_ref[...]
        hist_s[...] = cbuf_ref[...]

    cw = cw_ref[...]
    cbias = cb_ref[...]
    br = br_ref[...]
    bi = bi_ref[...]
    nlam = -lam_ref[...]
    sp = jnp.maximum(nlam, 0.0) + jnp.log1p(jnp.exp(-jnp.abs(nlam)))
    c1 = (-0.5 * LRU_C) * sp

    def gate_math(taps, rows):
        for j in range(nblk):
            ln = slice(j * BLOCK_B, (j + 1) * BLOCK_B)
            acc = cw[0:1, ln] * taps[0][:, ln]
            for k in range(1, CONV_W):
                acc = acc + cw[k:k + 1, ln] * taps[k][:, ln]
            xc = cbias[:, ln] + acc
            gm = _dot(xc.astype(BF16), wri_ref[j])
            th_r = jnp.tanh(0.5 * (gm[:, 0:BLOCK_B] + br[:, ln]))
            th_i = jnp.tanh(0.5 * (gm[:, BLOCK_B:2 * BLOCK_B] + bi[:, ln]))
            log_a = c1[:, ln] * th_r + c1[:, ln]
            a = jnp.exp(log_a)
            mag = _sqrt_nonneg(-jnp.tanh(log_a) * (a * a + 1.0))
            a_s[rows, ln] = a
            b_s[rows, ln] = mag * ((0.5 * th_i + 0.5) * xc)

    head = jnp.concatenate([hist_s[...], xb_ref[0:rc, :]], axis=0)
    gate_math([head[k * nb:k * nb + rc, :] for k in range(CONV_W)], pl.ds(0, rc))

    def gates(c, _):
        r0 = pl.multiple_of(c * rc, rc)
        taps = [xb_ref[pl.ds(r0 - (CONV_W - 1 - k) * nb, rc), :] for k in range(CONV_W)]
        gate_math(taps, pl.ds(r0, rc))
        return 0

    lax.fori_loop(1, tm // rc, gates, 0)
    hist_s[...] = xb_ref[tm - hist:tm, :]

    def scan(t, h):
        rows = pl.ds(pl.multiple_of(t * nb, 8), nb)
        h = a_s[rows, :] * h + b_s[rows, :]
        b_s[rows, :] = h
        return h

    h = lax.fori_loop(0, tl, scan, hc[...], unroll=8)
    hc[...] = h

    def outp(c, _):
        rows = pl.ds(pl.multiple_of(c * rc, rc), rc)
        y_ref[rows, :] = (b_s[rows, :] * _gelu(gt_ref[rows, :])).astype(BF16)
        return 0

    lax.fori_loop(0, tm // rc, outp, 0)

    @pl.when(i == nt - 1)
    def _():
        hout_ref[...] = h
        cout_ref[...] = hist_s[...]


def _lru_branch(z, conv_w, conv_b, wri, b_r, b_i, lam, h0, cbuf, prev, *, layer, nb, seq, tl,
                row0):
    tm = tl * nb
    nt = seq // tl
    rc = min(256, tm)
    hist = (CONV_W - 1) * nb
    nblk = min(4, N_BLOCKS_B)
    lw = nblk * BLOCK_B
    xoff = D_MODEL // lw
    rb0 = row0 // tm
    aliased = prev is not None
    col = lambda k, i: (0, k)
    lcol = lambda k, i: (layer, 0, k)
    in_specs = [
        pl.BlockSpec((tm, lw), lambda k, i: (rb0 + i, xoff + k)),
        pl.BlockSpec((tm, lw), lambda k, i: (rb0 + i, 2 * xoff + k)),
        pl.BlockSpec((None, CONV_W, lw), lcol),
        pl.BlockSpec((None, 1, lw), lcol),
        pl.BlockSpec((None, nblk, BLOCK_B, 2 * BLOCK_B), lambda k, i: (layer, k, 0, 0)),
        pl.BlockSpec((None, 1, lw), lcol),
        pl.BlockSpec((None, 1, lw), lcol),
        pl.BlockSpec((None, 1, lw), lcol),
        pl.BlockSpec((nb, lw), col),
        pl.BlockSpec((hist, lw), col),
    ]
    args = [z, z, conv_w, conv_b, wri, b_r, b_i, lam, h0, cbuf]
    aliases = {}
    if aliased:
        in_specs.append(_any_spec())
        args.append(prev)
        aliases = {10: 0}
    return pl.pallas_call(
        functools.partial(_lru_kernel, nb=nb, tl=tl, rc=rc, nblk=nblk, aliased=aliased),
        grid=(N_BLOCKS_B // nblk, nt),
        in_specs=in_specs,
        out_specs=[
            pl.BlockSpec((tm, lw), lambda k, i: (rb0 + i, k)),
            pl.BlockSpec((nb, lw), col),
            pl.BlockSpec((hist, lw), col),
        ],
        out_shape=[
            jax.ShapeDtypeStruct((z.shape[0], D_MODEL), BF16),
            jax.ShapeDtypeStruct((nb, D_MODEL), F32),
            jax.ShapeDtypeStruct((hist, D_MODEL), F32),
        ],
        scratch_shapes=[
            pltpu.VMEM((hist, lw), F32),
            pltpu.VMEM((tm, lw), F32),
            pltpu.VMEM((tm, lw), F32),
            pltpu.VMEM((nb, lw), F32),
        ],
        input_output_aliases=aliases,
        compiler_params=_cparams(("arbitrary", "arbitrary"), 48),
        name="lru_branch",
    )(*args)


ROW_TILE = 256


def _rows_scratch():
    return pltpu.VMEM((D_MODEL // LANES, ROW_TILE, LANES), F32)


def _rows_from_batch_major(scr, x_ref, nb):
    steps = x_ref.shape[1]
    for n in range(nb):
        for c in range(D_MODEL // LANES):
            scr[c, pl.ds(n, steps, stride=nb), :] = x_ref[n, :, c * LANES:(c + 1) * LANES]
    return jnp.concatenate([scr[c] for c in range(D_MODEL // LANES)], axis=1)


def _rows_to_batch_major(scr, x, o_ref, nb):
    steps = o_ref.shape[1]
    for c in range(D_MODEL // LANES):
        scr[c] = x[:, c * LANES:(c + 1) * LANES]
    for n in range(nb):
        for c in range(D_MODEL // LANES):
            o_ref[n, :, c * LANES:(c + 1) * LANES] = scr[c, pl.ds(n, steps, stride=nb), :]


def _rmsnorm_rows(x, g_ref):
    inv = lax.rsqrt(jnp.mean(x * x, axis=-1, keepdims=True) + EPS)
    return x * inv * g_ref[...]


def _prenorm_kernel(x_ref, g_ref, *rest, nb, aliased):
    if aliased:
        rest = rest[1:]
    o_ref, scr = rest
    x = _rows_from_batch_major(scr, x_ref, nb)
    o_ref[...] = _rmsnorm_rows(x, g_ref).astype(o_ref.dtype)


def _prenorm(x_bm, g, prev, *, layer, row0, total_rows):
    nb, seq, _ = x_bm.shape
    steps = ROW_TILE // nb
    rb0 = row0 // ROW_TILE
    in_specs = [pl.BlockSpec((nb, steps, D_MODEL), lambda i: (0, i, 0)),
                pl.BlockSpec((None, 1, D_MODEL), lambda i: (layer, 0, 0))]
    args = [x_bm, g]
    aliases = {}
    if prev is not None:
        in_specs.append(_any_spec())
        args.append(prev)
        aliases = {2: 0}
    return pl.pallas_call(
        functools.partial(_prenorm_kernel, nb=nb, aliased=prev is not None),
        grid=(seq // steps,),
        in_specs=in_specs,
        out_specs=pl.BlockSpec((ROW_TILE, D_MODEL), lambda i: (rb0 + i, 0)),
        out_shape=jax.ShapeDtypeStruct((total_rows, D_MODEL), BF16),
        scratch_shapes=[_rows_scratch()],
        input_output_aliases=aliases,
        compiler_params=_cparams(("arbitrary",), 40),
        name="prenorm",
    )(*args)


def _finish_kernel(x_ref, f_ref, g_ref, *rest, nb, x_bm, out_bm, with_h, n_alias):
    rest = list(rest)
    gn_ref = rest.pop(0) if with_h else None
    rest = rest[n_alias:]
    xo_ref = rest.pop(0)
    h_ref = rest.pop(0) if with_h else None
    x = _rows_from_batch_major(rest[0], x_ref, nb) if x_bm else x_ref[...]
    x = x + _rmsnorm_rows(f_ref[...], g_ref)
    if out_bm:
        _rows_to_batch_major(rest[0], x, xo_ref, nb)
    else:
        xo_ref[...] = x
    if with_h:
        h_ref[...] = _rmsnorm_rows(x, gn_ref).astype(h_ref.dtype)


def _finish(x, f, g, layer, g_next, layer_next, prev, *, nb, seq, row0, x_bm, out_bm):
    total_rows = f.shape[0]
    steps = ROW_TILE // nb
    rb0 = row0 // ROW_TILE
    with_h = g_next is not None
    row = pl.BlockSpec((ROW_TILE, D_MODEL), lambda i: (rb0 + i, 0))
    bm = pl.BlockSpec((nb, steps, D_MODEL), lambda i: (0, i, 0))
    vec = lambda l: pl.BlockSpec((None, 1, D_MODEL), lambda i: (l, 0, 0))
    in_specs = [bm if x_bm else row, row, vec(layer)]
    args = [x, f, g]
    if with_h:
        in_specs.append(vec(layer_next))
        args.append(g_next)
    aliases = {}
    if prev is not None:
        for k, p in enumerate(prev):
            aliases[len(args)] = k
            in_specs.append(_any_spec())
            args.append(p)
    out_specs = [bm if out_bm else row]
    out_shape = [jax.ShapeDtypeStruct((nb, seq, D_MODEL) if out_bm else (total_rows, D_MODEL), F32)]
    if with_h:
        out_specs.append(row)
        out_shape.append(jax.ShapeDtypeStruct((total_rows, D_MODEL), BF16))
    scratch = [_rows_scratch()] if (x_bm or out_bm) else []
    outs = pl.pallas_call(
        functools.partial(_finish_kernel, nb=nb, x_bm=x_bm, out_bm=out_bm, with_h=with_h,
                          n_alias=len(aliases)),
        grid=(seq * nb // ROW_TILE,),
        in_specs=in_specs,
        out_specs=out_specs,
        out_shape=out_shape,
        scratch_shapes=scratch,
        input_output_aliases=aliases,
        compiler_params=_cparams(("arbitrary",), 48),
        name="finish",
    )(*args)
    return (outs[0], outs[1]) if with_h else (outs[0], None)


def _mxu_weight(w_ref):
    return w_ref[...].astype(BF16)


def _mm_kernel(a_ref, w_ref, o_ref):
    o_ref[...] = _dot(a_ref[...], _mxu_weight(w_ref)).astype(o_ref.dtype)


def _matmul(a, w, *, layer, tm, tn, vmem_mib=56):
    t, k = a.shape
    n = w.shape[2]
    return pl.pallas_call(
        _mm_kernel,
        grid=(t // tm, n // tn),
        in_specs=[pl.BlockSpec((tm, k), lambda i, j: (i, 0)),
                  pl.BlockSpec((None, k, tn), lambda i, j: (layer, 0, j))],
        out_specs=pl.BlockSpec((tm, tn), lambda i, j: (i, j)),
        out_shape=jax.ShapeDtypeStruct((t, n), F32),
        compiler_params=_cparams(("arbitrary", "arbitrary"), vmem_mib),
        name="matmul",
    )(a, w)


def _glu_kernel(a_ref, w_ref, b_ref, y_ref, o_ref):
    gate = _sigmoid(_dot(a_ref[...], _mxu_weight(w_ref)) + b_ref[...])
    o_ref[...] = (y_ref[...] * gate).astype(o_ref.dtype)


def _glu(ya_b, ya_f, w, b, *, layer, tm, tn):
    t = ya_b.shape[0]
    return pl.pallas_call(
        _glu_kernel,
        grid=(t // tm, D_MODEL // tn),
        in_specs=[pl.BlockSpec((tm, D_MODEL), lambda i, j: (i, 0)),
                  pl.BlockSpec((None, D_MODEL, tn), lambda i, j: (layer, 0, j)),
                  pl.BlockSpec((None, 1, tn), lambda i, j: (layer, 0, j)),
                  pl.BlockSpec((tm, tn), lambda i, j: (i, j))],
        out_specs=pl.BlockSpec((tm, tn), lambda i, j: (i, j)),
        out_shape=jax.ShapeDtypeStruct((t, D_MODEL), BF16),
        compiler_params=_cparams(("arbitrary", "arbitrary"), 58),
        name="glu",
    )(ya_b, w, b, ya_f)


def _merge_kernel(a1_ref, w1_ref, a2_ref, w2_ref, ga_ref, gb_ref, o_ref):
    m = (_sigmoid(ga_ref[...]) * _dot(a1_ref[...], w1_ref[...])
         + _sigmoid(gb_ref[...]) * _dot(a2_ref[...], w2_ref[...]))
    o_ref[...] = m.astype(o_ref.dtype)


def _merge(ya, p_a, yb, p_b, z, *, layer, tm, tn):
    t = ya.shape[0]
    ga_off = 3 * D_MODEL // tn
    gb_off = 4 * D_MODEL // tn
    wspec = pl.BlockSpec((None, D_MODEL, tn), lambda i, j: (layer, 0, j))
    return pl.pallas_call(
        _merge_kernel,
        grid=(t // tm, D_MODEL // tn),
        in_specs=[pl.BlockSpec((tm, D_MODEL), lambda i, j: (i, 0)),
                  wspec,
                  pl.BlockSpec((tm, D_MODEL), lambda i, j: (i, 0)),
                  wspec,
                  pl.BlockSpec((tm, tn), lambda i, j: (i, ga_off + j)),
                  pl.BlockSpec((tm, tn), lambda i, j: (i, gb_off + j))],
        out_specs=pl.BlockSpec((tm, tn), lambda i, j: (i, j)),
        out_shape=jax.ShapeDtypeStruct((t, D_MODEL), BF16),
        compiler_params=_cparams(("arbitrary", "arbitrary"), 56),
        name="merge",
    )(ya, p_a, yb, p_b, z, z)


def _ffn_up_kernel(a_ref, wg_ref, wu_ref, o_ref):
    a = a_ref[...]
    g = _dot(a, _mxu_weight(wg_ref))
    u = _dot(a, _mxu_weight(wu_ref))
    o_ref[...] = (g * _sigmoid(g) * u).astype(o_ref.dtype)


def _ffn_up(h, wg, wu, *, layer, tm, tn):
    t = h.shape[0]
    wspec = pl.BlockSpec((None, D_MODEL, tn), lambda i, j: (layer, 0, j))
    return pl.pallas_call(
        _ffn_up_kernel,
        grid=(t // tm, D_FF // tn),
        in_specs=[pl.BlockSpec((tm, D_MODEL), lambda i, j: (i, 0)), wspec, wspec],
        out_specs=pl.BlockSpec((tm, tn), lambda i, j: (i, j)),
        out_shape=jax.ShapeDtypeStruct((t, D_FF), BF16),
        compiler_params=_cparams(("arbitrary", "arbitrary"), 56),
        name="ffn_up",
    )(h, wg, wu)


def kernel(x_prompt, x_sample, state_s5_re, state_s5_im, state_lru, cache_conv, g_pre_mix, w_in, s5_lam_re, s5_lam_im, s5_log_step, s5_b_re, s5_b_im, s5_c_re, s5_c_im, s5_d, w_glu, b_glu, conv_w, conv_b, lru_w_r, lru_b_r, lru_w_i, lru_b_i, lru_lam, p_a, p_b, w_out, g_post_mix, g_pre_ffn, w_gate, w_up, w_down, g_post_ffn):
    depth = w_in.shape[0]
    nbp, lp, _ = x_prompt.shape
    nbs, ls, _ = x_sample.shape
    sd = state_lru.dtype
    rows_p = nbp * lp
    rows_s = nbs * ls
    hist_p = (CONV_W - 1) * nbp
    hist_s = (CONV_W - 1) * nbs
    rows = rows_p + rows_s
    tm = 1280 if rows % 1280 == 0 else 256
    tm_half = tm // 2
    tl_s5 = min(512, lp)
    tl_lru = min(128, lp)
    seg_p = dict(nb=nbp, seq=lp, row0=0)
    seg_s = dict(nb=nbs, seq=ls, row0=rows_p)

    kb, ct, pp, qq = _s5_prep(s5_lam_re, s5_lam_im, s5_log_step,
                              s5_b_re.transpose(0, 1, 3, 2), s5_b_im.transpose(0, 1, 3, 2),
                              s5_c_re, s5_c_im)
    vec = lambda a: a.reshape(depth, 1, -1)
    w_in_b, w_glu_b, w_out_b, w_gate_b, w_up_b = w_in, w_glu, w_out, w_gate, w_up
    p_a_b = p_a.astype(BF16)
    p_b_b = p_b.astype(BF16)
    w_down_b = w_down.astype(BF16)
    wri = jnp.concatenate([lru_w_r, lru_w_i], axis=-1).astype(BF16)
    g_pre_mix, g_post_mix, g_pre_ffn, g_post_ffn = map(
        vec, (g_pre_mix, g_post_mix, g_pre_ffn, g_post_ffn))
    s5_d, b_glu, conv_b, lru_b_r, lru_b_i, lru_lam = map(
        vec, (s5_d, b_glu, conv_b, lru_b_r, lru_b_i, lru_lam))

    h = _prenorm(x_prompt, g_pre_mix, None, layer=0, row0=0, total_rows=rows)
    h = _prenorm(x_sample, g_pre_mix, h, layer=0, row0=rows_p, total_rows=rows)
    zeros_s5 = jnp.zeros((N_GROUPS, nbp, 2 * P_STATE), F32)
    zeros_lru = jnp.zeros((nbp, D_MODEL), F32)
    zeros_conv = jnp.zeros((hist_p, D_MODEL), F32)
    outs_p, outs_s = [], []
    for l in range(depth):
        z = _matmul(h, w_in_b, layer=l, tm=tm, tn=512)

        ya_f, ya_b, s5p = _s5_branch(z, kb, ct, pp, qq, s5_d, zeros_s5, None,
                                     layer=l, nb=nbp, seq=lp, tl=tl_s5, row0=0)
        s5_h0 =jnp.concatenate([state_s5_re[l], state_s5_im[l]], axis=-1).transpose(1, 0, 2)
        ya_f, ya_b, s5s = _s5_branch(z, kb, ct, pp, qq, s5_d, s5_h0.astype(F32), (ya_f, ya_b),
                                     layer=l, nb=nbs, seq=ls, tl=ls, row0=rows_p)
        lru_args = (conv_w, conv_b, wri, lru_b_r, lru_b_i, lru_lam)
        yb, lrup, convp = _lru_branch(z, *lru_args, zeros_lru, zeros_conv, None,
                                      layer=l, nb=nbp, seq=lp, tl=tl_lru, row0=0)
        cbuf = cache_conv[l].transpose(1, 0, 2).reshape(hist_s, D_MODEL)
        yb, lrus, convs = _lru_branch(z, *lru_args, state_lru[l].astype(F32), cbuf.astype(F32),
                                      yb, layer=l, nb=nbs, seq=ls, tl=ls, row0=rows_p)
        outs_p.append((s5p, lrup, convp))
        outs_s.append((s5s, lrus, convs))

        ya2 = _glu(ya_b, ya_f, w_glu_b, b_glu, layer=l, tm=tm, tn=512)
        merged = _merge(ya2, p_a_b, yb, p_b_b, z, layer=l, tm=tm_half, tn=512)
        f = _matmul(merged, w_out_b, layer=l, tm=tm, tn=512)
        if l == 0:
            x, h = _finish(x_prompt, f, g_post_mix, l, g_pre_ffn, l, None,
                           x_bm=True, out_bm=False, **seg_p)
            x, h = _finish(x_sample, f, g_post_mix, l, g_pre_ffn, l, (x, h),
                           x_bm=True, out_bm=False, **seg_s)
        else:
            x, h = _finish(x, f, g_post_mix, l, g_pre_ffn, l, None,
                           nb=1, seq=rows, row0=0, x_bm=False, out_bm=False)
        hmid = _ffn_up(h, w_gate_b, w_up_b, layer=l, tm=tm, tn=256)
        f = _matmul(hmid, w_down_b, layer=l, tm=tm_half, tn=256)
        if l + 1 < depth:
            x, h = _finish(x, f, g_post_ffn, l, g_pre_mix, l + 1, None,
                           nb=1, seq=rows, row0=0, x_bm=False, out_bm=False)
        else:
            y_prompt, _ = _finish(x, f, g_post_ffn, l, None, None, None,
                                  x_bm=False, out_bm=True, **seg_p)
            y_sample, _ = _finish(x, f, g_post_ffn, l, None, None, None,
                                  x_bm=False, out_bm=True, **seg_s)

    def pack(outs, nb):
        s5 = jnp.stack([o[0] for o in outs]).transpose(0, 2, 1, 3)
        lru = jnp.stack([o[1] for o in outs])
        conv = jnp.stack([o[2] for o in outs]).reshape(depth, CONV_W - 1, nb, D_MODEL)
        return (s5[..., :P_STATE].astype(sd), s5[..., P_STATE:].astype(sd), lru.astype(sd),
                conv.transpose(0, 2, 1, 3).astype(sd))

    p_re, p_im, p_lru, p_conv = pack(outs_p, nbp)
    s_re, s_im, s_lru, s_conv = pack(outs_s, nbs)
    return (y_prompt, y_sample, p_re, p_im, p_lru, p_conv, s_re, s_im, s_lru, s_conv)
```

```python
import functools
import math

import jax
import jax.numpy as jnp
from jax import lax
from jax.experimental import pallas as pl
from jax.experimental.pallas import tpu as pltpu

F32 = jnp.float32
BF16 = jnp.bfloat16

D_MODEL = 4096
S5_GROUP = 16
N_GROUPS = D_MODEL // S5_GROUP
P_STATE = 64
S5_CHUNK = 16
N_BLOCKS_B = 16
BLOCK_B = D_MODEL // N_BLOCKS_B
CONV_W = 4
LRU_C = 8.0
D_FF = 11008
EPS = 1e-6
LANES = 128
MIB = 1024 * 1024


def _cparams(sem, vmem_mib):
    return pltpu.CompilerParams(dimension_semantics=sem, vmem_limit_bytes=vmem_mib * MIB)


def _sigmoid(x):
    return 0.5 * (jnp.tanh(0.5 * x) + 1.0)


def _gelu(x):
    c = math.sqrt(2.0 / math.pi)
    hx = 0.5 * x
    return hx + hx * jnp.tanh(x * (c + (c * 0.044715) * (x * x)))


def _sqrt_nonneg(v):
    return jnp.where(v == 0.0, 0.0, v * lax.rsqrt(v))


def _dot(a, b):
    return jnp.dot(a, b, preferred_element_type=F32)


def _dot_nt(a, b, precision=None):
    return lax.dot_general(a, b, (((1,), (1,)), ((), ())), precision=precision,
                           preferred_element_type=F32)


def _any_spec():
    return pl.BlockSpec(memory_space=pl.ANY)


def _s5_prep_kernel(lre_ref, lim_ref, ls_ref, bre_ref, bim_ref, cre_ref, cim_ref,
                    kb_ref, ct_ref, p_ref, q_ref, *, gb):
    tau = lax.broadcasted_iota(jnp.int32, (24, P_STATE), 0).astype(F32)
    lane = lax.broadcasted_iota(jnp.int32, (S5_GROUP, 2 * LANES), 1)
    for g in range(gb):
        lr = lre_ref[0, g:g + 1, :]
        li = lim_ref[0, g:g + 1, :]
        step = jnp.exp(ls_ref[0, g:g + 1, :])
        xr = lr * step
        xi = li * step
        mag = jnp.exp(xr * tau)
        ang = xi * tau
        pw_r = mag * jnp.cos(ang)
        pw_i = mag * jnp.sin(ang)
        a_r = pw_r[1:2]
        a_i = pw_i[1:2]
        den = lr * lr + li * li
        n_r = a_r - 1.0
        n_i = a_i
        co_r = (n_r * lr + n_i * li) / den
        co_i = (n_i * lr - n_r * li) / den
        bt_r = bre_ref[0, g]
        bt_i = bim_ref[0, g]
        bb_r = co_r * bt_r - co_i * bt_i
        bb_i = co_r * bt_i + co_i * bt_r
        c_r = cre_ref[0, g]
        c_i = cim_ref[0, g]
        ca = []
        for t in range(S5_CHUNK + 1):
            e_r = pw_r[t:t + 1]
            e_i = pw_i[t:t + 1]
            re = c_r * e_r - c_i * e_i
            im = c_r * e_i + c_i * e_r
            ca.append(jnp.concatenate([re, -im], axis=1))
        ct_ref[0, g] = jnp.concatenate(ca[1:], axis=0).T.astype(BF16)
        ca_all = jnp.concatenate(ca[:S5_CHUNK], axis=0)
        b2 = jnp.concatenate([bb_r, bb_i], axis=1)
        strip = _dot_nt(b2, ca_all, precision=lax.Precision.HIGHEST)
        for s in range(S5_CHUNK):
            if s == 0:
                blk = strip
            else:
                blk = jnp.where(lane >= 16 * s, pltpu.roll(strip, 16 * s, 1), 0.0)
            kb_ref[0, g, s * 16:(s + 1) * 16, 0:256] = blk.astype(BF16)
            e_r = pw_r[15 - s:16 - s]
            e_i = pw_i[15 - s:16 - s]
            bm_r = bb_r * e_r - bb_i * e_i
            bm_i = bb_r * e_i + bb_i * e_r
            kb_ref[0, g, s * 16:(s + 1) * 16, 256:512] = jnp.concatenate(
                [bm_r, bm_i, bm_i, bm_r], axis=1).astype(BF16)
        a16_r = pw_r[16:17]
        a16_i = pw_i[16:17]
        p_ref[0, g:g + 1, :] = jnp.concatenate([a16_r, a16_r], axis=1)
        q_ref[0, g:g + 1, :] = jnp.concatenate([-a16_i, a16_i], axis=1)


def _s5_prep(lam_re, lam_im, log_step, bt_re, bt_im, c_re, c_im):
    depth = lam_re.shape[0]
    gb = 8
    grid = (depth, N_GROUPS // gb)
    m3 = lambda l, g: (l, g, 0)
    m4 = lambda l, g: (l, g, 0, 0)
    return pl.pallas_call(
        functools.partial(_s5_prep_kernel, gb=gb),
        grid=grid,
        in_specs=[
            pl.BlockSpec((1, gb, P_STATE), m3),
            pl.BlockSpec((1, gb, P_STATE), m3),
            pl.BlockSpec((1, gb, 1), m3),
            pl.BlockSpec((1, gb, S5_GROUP, P_STATE), m4),
            pl.BlockSpec((1, gb, S5_GROUP, P_STATE), m4),
            pl.BlockSpec((1, gb, S5_GROUP, P_STATE), m4),
            pl.BlockSpec((1, gb, S5_GROUP, P_STATE), m4),
        ],
        out_specs=[
            pl.BlockSpec((1, gb, 256, 512), m4),
            pl.BlockSpec((1, gb, 128, 256), m4),
            pl.BlockSpec((1, gb, 128), m3),
            pl.BlockSpec((1, gb, 128), m3),
        ],
        out_shape=[
            jax.ShapeDtypeStruct((depth, N_GROUPS, 256, 512), BF16),
            jax.ShapeDtypeStruct((depth, N_GROUPS, 128, 256), BF16),
            jax.ShapeDtypeStruct((depth, N_GROUPS, 128), F32),
            jax.ShapeDtypeStruct((depth, N_GROUPS, 128), F32),
        ],
        compiler_params=_cparams(("arbitrary", "arbitrary"), 32),
        name="s5_prep",
    )(lam_re, lam_im, log_step.reshape(depth, N_GROUPS, 1), bt_re, bt_im, c_re, c_im)


def _block_transpose8(arrs, masks):
    arrs = list(arrs)
    for k in range(3):
        d = 1 << k
        sh = 16 * d
        new = list(arrs)
        for a in range(8):
            if a & d:
                continue
            b = a | d
            va, vb = arrs[a], arrs[b]
            new[a] = jnp.where(masks[k], pltpu.roll(vb, sh, 1), va)
            new[b] = jnp.where(masks[k], vb, pltpu.roll(va, LANES - sh, 1))
        arrs = new
    return arrs


def _lane_block_masks(rows):
    lane_blk = lax.broadcasted_iota(jnp.int32, (rows, LANES), 1) // 16
    return [((lane_blk >> k) & 1) == 1 for k in range(3)]


def _s5_kernel(u_ref, kb_ref, ct_ref, p_ref, q_ref, d_ref, h0_ref, *rest, nb, tl, gps, aliased):
    if aliased:
        rest = rest[2:]
    yf_ref, yb_ref, hout_ref, uflat, hb, hbs, hprev, ybuf, carry = rest
    i = pl.program_id(1)
    nt = pl.num_programs(1)
    nc = tl // S5_CHUNK
    cb = min(32, nc)
    nsub = nc // cb
    sr = cb * nb
    nh = gps // 8

    @pl.when(i == 0)
    def _():
        carry[...] = h0_ref[...]

    masks_f = _lane_block_masks(sr)
    masks_p = _lane_block_masks(sr // 2)

    def fwd(j, _):
        base = pl.multiple_of(j * (cb * S5_CHUNK * nb), 8)
        r0 = pl.multiple_of(j * sr, 16)
        for hh in range(nh):
            for sh in range(2):
                arrs = []
                for sl in range(8):
                    s = sh * 8 + sl
                    pieces = [u_ref[pl.ds(base + (cc * S5_CHUNK + s) * nb, nb),
                                    hh * LANES:(hh + 1) * LANES] for cc in range(cb)]
                    v = pieces[0] if cb == 1 else jnp.concatenate(pieces, axis=0)
                    arrs.append(pltpu.bitcast(v.astype(BF16), jnp.uint32))
                w = _block_transpose8(arrs, masks_p)
                for g in range(8):
                    uflat[hh * 8 + g, pl.ds(r0, sr), sh * LANES:(sh + 1) * LANES] = (
                        pltpu.bitcast(w[g], BF16))
        return 0

    lax.fori_loop(0, nsub, fwd, 0)

    for gi in range(gps):
        hbg = _dot(uflat[gi], kb_ref[gi, :, 256:512])
        hb[gi] = hbg[:, 0:LANES]
        hbs[gi] = hbg[:, LANES:2 * LANES]

    def step(c, hs):
        rows = pl.ds(pl.multiple_of(c * nb, 8), nb)
        out = []
        for gi in range(gps):
            h, hsw = hs[2 * gi], hs[2 * gi + 1]
            hprev[gi, rows, :] = h
            pg = p_ref[gi:gi + 1, :]
            qg = q_ref[gi:gi + 1, :]
            out.append(pg * h + qg * hsw + hb[gi, rows, :])
            out.append(pg * hsw - qg * h + hbs[gi, rows, :])
        return tuple(out)

    init = []
    for gi in range(gps):
        h = carry[gi]
        init += [h, pltpu.roll(h, P_STATE, 1)]
    hs = lax.fori_loop(0, nc, step, tuple(init))
    for gi in range(gps):
        carry[gi] = hs[2 * gi]

    for gi in range(gps):
        ybuf[gi] = (_dot(uflat[gi], kb_ref[gi, :, 0:256])
                    + _dot(hprev[gi].astype(BF16), ct_ref[gi]))

    def inv(j, _):
        base = pl.multiple_of(j * (cb * S5_CHUNK * nb), 8)
        r0 = pl.multiple_of(j * sr, 16)
        for hh in range(nh):
            lanes = slice(hh * LANES, (hh + 1) * LANES)
            dsk = d_ref[:, lanes]
            for sh in range(2):
                arrs = [ybuf[hh * 8 + g, pl.ds(r0, sr), sh * LANES:(sh + 1) * LANES]
                        for g in range(8)]
                z = _block_transpose8(arrs, masks_f)
                for sl in range(0, 8, 2):
                    t = sh * 8 + sl
                    for cc in range(cb):
                        rows = pl.ds(base + (cc * S5_CHUNK + t) * nb, 2 * nb)
                        y2 = jnp.concatenate([z[sl][cc * nb:(cc + 1) * nb, :],
                                              z[sl + 1][cc * nb:(cc + 1) * nb, :]], axis=0)
                        ya = _gelu(y2 + dsk * u_ref[rows, lanes])
                        yf_ref[rows, lanes] = ya
                        yb_ref[rows, lanes] = ya.astype(BF16)
        return 0

    lax.fori_loop(0, nsub, inv, 0)

    @pl.when(i == nt - 1)
    def _():
        hout_ref[...] = carry[...]


def _s5_branch(z, kb, ct, pp, qq, dskip, h0, prev, *, layer, nb, seq, tl, row0):
    gps = 16
    tm = tl * nb
    nt = seq // tl
    nblk = N_GROUPS // gps
    cw = gps * S5_GROUP
    r = (tl // S5_CHUNK) * nb
    rb0 = row0 // tm
    t_rows = z.shape[0]
    aliased = prev is not None
    in_specs = [
        pl.BlockSpec((tm, cw), lambda b, i: (rb0 + i, b)),
        pl.BlockSpec((None, gps, 256, 512), lambda b, i: (layer, b, 0, 0)),
        pl.BlockSpec((None, gps, 128, 256), lambda b, i: (layer, b, 0, 0)),
        pl.BlockSpec((None, gps, 128), lambda b, i: (layer, b, 0)),
        pl.BlockSpec((None, gps, 128), lambda b, i: (layer, b, 0)),
        pl.BlockSpec((None, 1, cw), lambda b, i: (layer, 0, b)),
        pl.BlockSpec((gps, nb, 128), lambda b, i: (b, 0, 0)),
    ]
    args = [z, kb, ct, pp, qq, dskip, h0]
    aliases = {}
    if aliased:
        in_specs += [_any_spec(), _any_spec()]
        args += list(prev)
        aliases = {7: 0, 8: 1}
    return pl.pallas_call(
        functools.partial(_s5_kernel, nb=nb, tl=tl, gps=gps, aliased=aliased),
        grid=(nblk, nt),
        in_specs=in_specs,
        out_specs=[
            pl.BlockSpec((tm, cw), lambda b, i: (rb0 + i, b)),
            pl.BlockSpec((tm, cw), lambda b, i: (rb0 + i, b)),
            pl.BlockSpec((gps, nb, 128), lambda b, i: (b, 0, 0)),
        ],
        out_shape=[
            jax.ShapeDtypeStruct((t_rows, D_MODEL), F32),
            jax.ShapeDtypeStruct((t_rows, D_MODEL), BF16),
            jax.ShapeDtypeStruct((N_GROUPS, nb, 128), F32),
        ],
        scratch_shapes=[
            pltpu.VMEM((gps, r, 256), BF16),
            pltpu.VMEM((gps, r, 128), F32),
            pltpu.VMEM((gps, r, 128), F32),
            pltpu.VMEM((gps, r, 128), F32),
            pltpu.VMEM((gps, r, 256), F32),
            pltpu.VMEM((gps, nb, 128), F32),
        ],
        input_output_aliases=aliases,
        compiler_params=_cparams(("arbitrary", "arbitrary"), 58),
        name="s5_branch",
    )(*args)


def _lru_kernel(xb_ref, gt_ref, cw_ref, cb_ref, wri_ref, br_ref, bi_ref, lam_ref,
                h0_ref, cbuf_ref, *rest, nb, tl, rc, nblk, aliased):
    if aliased:
        rest = rest[1:]
    y_ref, hout_ref, cout_ref, hist_s, a_s, b_s, hc = rest
    i = pl.program_id(1)
    nt = pl.num_programs(1)
    tm = tl * nb
    hist = (CONV_W - 1) * nb

    @pl.when(i == 0)
    def _():
        hc[...] = h0_ref[...]
        hist_s[...] = cbuf_ref[...]

    cw = cw_ref[...]
    cbias = cb_ref[...]
    br = br_ref[...]
    bi = bi_ref[...]
    nlam = -lam_ref[...]
    sp = jnp.maximum(nlam, 0.0) + jnp.log1p(jnp.exp(-jnp.abs(nlam)))
    c1 = (-0.5 * LRU_C) * sp

    def gate_math(taps, rows):
        for j in range(nblk):
            ln = slice(j * BLOCK_B, (j + 1) * BLOCK_B)
            acc = cw[0:1, ln] * taps[0][:, ln]
            for k in range(1, CONV_W):
                acc = acc + cw[k:k + 1, ln] * taps[k][:, ln]
            xc = cbias[:, ln] + acc
            gm = _dot(xc.astype(BF16), wri_ref[j])
            th_r = jnp.tanh(0.5 * (gm[:, 0:BLOCK_B] + br[:, ln]))
            th_i = jnp.tanh(0.5 * (gm[:, BLOCK_B:2 * BLOCK_B] + bi[:, ln]))
            log_a = c1[:, ln] * th_r + c1[:, ln]
            a = jnp.exp(log_a)
            mag = _sqrt_nonneg(-jnp.tanh(log_a) * (a * a + 1.0))
            a_s[rows, ln] = a
            b_s[rows, ln] = mag * ((0.5 * th_i + 0.5) * xc)

    head = jnp.concatenate([hist_s[...], xb_ref[0:rc, :]], axis=0)
    gate_math([head[k * nb:k * nb + rc, :] for k in range(CONV_W)], pl.ds(0, rc))

    def gates(c, _):
        r0 = pl.multiple_of(c * rc, rc)
        taps = [xb_ref[pl.ds(r0 - (CONV_W - 1 - k) * nb, rc), :] for k in range(CONV_W)]
        gate_math(taps, pl.ds(r0, rc))
        return 0

    lax.fori_loop(1, tm // rc, gates, 0)
    hist_s[...] = xb_ref[tm - hist:tm, :]

    def scan(t2, h):
        r0 = pl.multiple_of(t2 * (2 * nb), 16)
        h1 = a_s[pl.ds(r0, nb), :] * h + b_s[pl.ds(r0, nb), :]
        h2 = a_s[pl.ds(r0 + nb, nb), :] * h1 + b_s[pl.ds(r0 + nb, nb), :]
        rows = pl.ds(r0, 2 * nb)
        hh = jnp.concatenate([h1, h2], axis=0)
        y_ref[rows, :] = (hh * _gelu(gt_ref[rows, :])).astype(BF16)
        return h2

    h = lax.fori_loop(0, tl // 2, scan, hc[...], unroll=4)
    hc[...] = h

    @pl.when(i == nt - 1)
    def _():
        hout_ref[...] = h
        cout_ref[...] = hist_s[...]


def _lru_branch(z, conv_w, conv_b, wri, b_r, b_i, lam, h0, cbuf, prev, *, layer, nb, seq, tl,
                row0):
    tm = tl * nb
    nt = seq // tl
    rc = min(256, tm)
    hist = (CONV_W - 1) * nb
    nblk = min(4, N_BLOCKS_B)
    lw = nblk * BLOCK_B
    xoff = D_MODEL // lw
    rb0 = row0 // tm
    aliased = prev is not None
    col = lambda k, i: (0, k)
    lcol = lambda k, i: (layer, 0, k)
    in_specs = [
        pl.BlockSpec((tm, lw), lambda k, i: (rb0 + i, xoff + k)),
        pl.BlockSpec((tm, lw), lambda k, i: (rb0 + i, 2 * xoff + k)),
        pl.BlockSpec((None, CONV_W, lw), lcol),
        pl.BlockSpec((None, 1, lw), lcol),
        pl.BlockSpec((None, nblk, BLOCK_B, 2 * BLOCK_B), lambda k, i: (layer, k, 0, 0)),
        pl.BlockSpec((None, 1, lw), lcol),
        pl.BlockSpec((None, 1, lw), lcol),
        pl.BlockSpec((None, 1, lw), lcol),
        pl.BlockSpec((nb, lw), col),
        pl.BlockSpec((hist, lw), col),
    ]
    args = [z, z, conv_w, conv_b, wri, b_r, b_i, lam, h0, cbuf]
    aliases = {}
    if aliased:
        in_specs.append(_any_spec())
        args.append(prev)
        aliases = {10: 0}
    return pl.pallas_call(
        functools.partial(_lru_kernel, nb=nb, tl=tl, rc=rc, nblk=nblk, aliased=aliased),
        grid=(N_BLOCKS_B // nblk, nt),
        in_specs=in_specs,
        out_specs=[
            pl.BlockSpec((tm, lw), lambda k, i: (rb0 + i, k)),
            pl.BlockSpec((nb, lw), col),
            pl.BlockSpec((hist, lw), col),
        ],
        out_shape=[
            jax.ShapeDtypeStruct((z.shape[0], D_MODEL), BF16),
            jax.ShapeDtypeStruct((nb, D_MODEL), F32),
            jax.ShapeDtypeStruct((hist, D_MODEL), F32),
        ],
        scratch_shapes=[
            pltpu.VMEM((hist, lw), F32),
            pltpu.VMEM((tm, lw), F32),
            pltpu.VMEM((tm, lw), F32),
            pltpu.VMEM((nb, lw), F32),
        ],
        input_output_aliases=aliases,
        compiler_params=_cparams(("arbitrary", "arbitrary"), 48),
        name="lru_branch",
    )(*args)


ROW_TILE = 256


def _rows_scratch():
    return pltpu.VMEM((D_MODEL // LANES, ROW_TILE, LANES), F32)


def _rows_from_batch_major(scr, x_ref, nb):
    steps = x_ref.shape[1]
    for n in range(nb):
        for c in range(D_MODEL // LANES):
            scr[c, pl.ds(n, steps, stride=nb), :] = x_ref[n, :, c * LANES:(c + 1) * LANES]
    return jnp.concatenate([scr[c] for c in range(D_MODEL // LANES)], axis=1)


def _rows_to_batch_major(scr, x, o_ref, nb):
    steps = o_ref.shape[1]
    for c in range(D_MODEL // LANES):
        scr[c] = x[:, c * LANES:(c + 1) * LANES]
    for n in range(nb):
        for c in range(D_MODEL // LANES):
            o_ref[n, :, c * LANES:(c + 1) * LANES] = scr[c, pl.ds(n, steps, stride=nb), :]


def _rmsnorm_rows(x, g_ref):
    inv = lax.rsqrt(jnp.mean(x * x, axis=-1, keepdims=True) + EPS)
    return x * inv * g_ref[...]


def _prenorm_kernel(x_ref, g_ref, *rest, nb, aliased):
    if aliased:
        rest = rest[1:]
    o_ref, scr = rest
    x = _rows_from_batch_major(scr, x_ref, nb)
    o_ref[...] = _rmsnorm_rows(x, g_ref).astype(o_ref.dtype)


def _prenorm(x_bm, g, prev, *, layer, row0, total_rows):
    nb, seq, _ = x_bm.shape
    steps = ROW_TILE // nb
    rb0 = row0 // ROW_TILE
    in_specs = [pl.BlockSpec((nb, steps, D_MODEL), lambda i: (0, i, 0)),
                pl.BlockSpec((None, 1, D_MODEL), lambda i: (layer, 0, 0))]
    args = [x_bm, g]
    aliases = {}
    if prev is not None:
        in_specs.append(_any_spec())
        args.append(prev)
        aliases = {2: 0}
    return pl.pallas_call(
        functools.partial(_prenorm_kernel, nb=nb, aliased=prev is not None),
        grid=(seq // steps,),
        in_specs=in_specs,
        out_specs=pl.BlockSpec((ROW_TILE, D_MODEL), lambda i: (rb0 + i, 0)),
        out_shape=jax.ShapeDtypeStruct((total_rows, D_MODEL), BF16),
        scratch_shapes=[_rows_scratch()],
        input_output_aliases=aliases,
        compiler_params=_cparams(("arbitrary",), 40),
        name="prenorm",
    )(*args)


def _finish_kernel(x_ref, f_ref, g_ref, *rest, nb, x_bm, out_bm, with_h, n_alias):
    rest = list(rest)
    gn_ref = rest.pop(0) if with_h else None
    rest = rest[n_alias:]
    xo_ref = rest.pop(0)
    h_ref = rest.pop(0) if with_h else None
    x = _rows_from_batch_major(rest[0], x_ref, nb) if x_bm else x_ref[...]
    x = x + _rmsnorm_rows(f_ref[...], g_ref)
    if out_bm:
        _rows_to_batch_major(rest[0], x, xo_ref, nb)
    else:
        xo_ref[...] = x
    if with_h:
        h_ref[...] = _rmsnorm_rows(x, gn_ref).astype(h_ref.dtype)


def _finish(x, f, g, layer, g_next, layer_next, prev, *, nb, seq, row0, x_bm, out_bm):
    total_rows = f.shape[0]
    steps = ROW_TILE // nb
    rb0 = row0 // ROW_TILE
    with_h = g_next is not None
    row = pl.BlockSpec((ROW_TILE, D_MODEL), lambda i: (rb0 + i, 0))
    bm = pl.BlockSpec((nb, steps, D_MODEL), lambda i: (0, i, 0))
    vec = lambda l: pl.BlockSpec((None, 1, D_MODEL), lambda i: (l, 0, 0))
    in_specs = [bm if x_bm else row, row, vec(layer)]
    args = [x, f, g]
    if with_h:
        in_specs.append(vec(layer_next))
        args.append(g_next)
    aliases = {}
    if prev is not None:
        for k, p in enumerate(prev):
            aliases[len(args)] = k
            in_specs.append(_any_spec())
            args.append(p)
    out_specs = [bm if out_bm else row]
    out_shape = [jax.ShapeDtypeStruct((nb, seq, D_MODEL) if out_bm else (total_rows, D_MODEL), F32)]
    if with_h:
        out_specs.append(row)
        out_shape.append(jax.ShapeDtypeStruct((total_rows, D_MODEL), BF16))
    scratch = [_rows_scratch()] if (x_bm or out_bm) else []
    outs = pl.pallas_call(
        functools.partial(_finish_kernel, nb=nb, x_bm=x_bm, out_bm=out_bm, with_h=with_h,
                          n_alias=len(aliases)),
        grid=(seq * nb // ROW_TILE,),
        in_specs=in_specs,
        out_specs=out_specs,
        out_shape=out_shape,
        scratch_shapes=scratch,
        input_output_aliases=aliases,
        compiler_params=_cparams(("arbitrary",), 48),
        name="finish",
    )(*args)
    return (outs[0], outs[1]) if with_h else (outs[0], None)


def _mxu_weight(w_ref):
    return w_ref[...].astype(BF16)


def _mm_kernel(a_ref, w_ref, o_ref):
    o_ref[...] = _dot(a_ref[...], _mxu_weight(w_ref)).astype(o_ref.dtype)


def _matmul(a, w, *, layer, tm, tn, vmem_mib=56):
    t, k = a.shape
    n = w.shape[2]
    return pl.pallas_call(
        _mm_kernel,
        grid=(t // tm, n // tn),
        in_specs=[pl.BlockSpec((tm, k), lambda i, j: (i, 0)),
                  pl.BlockSpec((None, k, tn), lambda i, j: (layer, 0, j))],
        out_specs=pl.BlockSpec((tm, tn), lambda i, j: (i, j)),
        out_shape=jax.ShapeDtypeStruct((t, n), F32),
        compiler_params=_cparams(("arbitrary", "arbitrary"), vmem_mib),
        name="matmul",
    )(a, w)


def _glu_kernel(a_ref, w_ref, b_ref, y_ref, o_ref):
    gate = _sigmoid(_dot(a_ref[...], _mxu_weight(w_ref)) + b_ref[...])
    o_ref[...] = (y_ref[...] * gate).astype(o_ref.dtype)


def _glu(ya_b, ya_f, w, b, *, layer, tm, tn):
    t = ya_b.shape[0]
    return pl.pallas_call(
        _glu_kernel,
        grid=(t // tm, D_MODEL // tn),
        in_specs=[pl.BlockSpec((tm, D_MODEL), lambda i, j: (i, 0)),
                  pl.BlockSpec((None, D_MODEL, tn), lambda i, j: (layer, 0, j)),
                  pl.BlockSpec((None, 1, tn), lambda i, j: (layer, 0, j)),
                  pl.BlockSpec((tm, tn), lambda i, j: (i, j))],
        out_specs=pl.BlockSpec((tm, tn), lambda i, j: (i, j)),
        out_shape=jax.ShapeDtypeStruct((t, D_MODEL), BF16),
        compiler_params=_cparams(("arbitrary", "arbitrary"), 58),
        name="glu",
    )(ya_b, w, b, ya_f)


def _merge_kernel(a1_ref, w1_ref, a2_ref, w2_ref, ga_ref, gb_ref, o_ref):
    m = (_sigmoid(ga_ref[...]) * _dot(a1_ref[...], w1_ref[...])
         + _sigmoid(gb_ref[...]) * _dot(a2_ref[...], w2_ref[...]))
    o_ref[...] = m.astype(o_ref.dtype)


def _merge(ya, p_a, yb, p_b, z, *, layer, tm, tn):
    t = ya.shape[0]
    ga_off = 3 * D_MODEL // tn
    gb_off = 4 * D_MODEL // tn
    wspec = pl.BlockSpec((None, D_MODEL, tn), lambda i, j: (layer, 0, j))
    return pl.pallas_call(
        _merge_kernel,
        grid=(t // tm, D_MODEL // tn),
        in_specs=[pl.BlockSpec((tm, D_MODEL), lambda i, j: (i, 0)),
                  wspec,
                  pl.BlockSpec((tm, D_MODEL), lambda i, j: (i, 0)),
                  wspec,
                  pl.BlockSpec((tm, tn), lambda i, j: (i, ga_off + j)),
                  pl.BlockSpec((tm, tn), lambda i, j: (i, gb_off + j))],
        out_specs=pl.BlockSpec((tm, tn), lambda i, j: (i, j)),
        out_shape=jax.ShapeDtypeStruct((t, D_MODEL), BF16),
        compiler_params=_cparams(("arbitrary", "arbitrary"), 56),
        name="merge",
    )(ya, p_a, yb, p_b, z, z)


def _ffn_up_kernel(a_ref, wg_ref, wu_ref, o_ref):
    a = a_ref[...]
    g = _dot(a, _mxu_weight(wg_ref))
    u = _dot(a, _mxu_weight(wu_ref))
    o_ref[...] = (g * _sigmoid(g) * u).astype(o_ref.dtype)


def _ffn_up(h, wg, wu, *, layer, tm, tn):
    t = h.shape[0]
    wspec = pl.BlockSpec((None, D_MODEL, tn), lambda i, j: (layer, 0, j))
    return pl.pallas_call(
        _ffn_up_kernel,
        grid=(t // tm, D_FF // tn),
        in_specs=[pl.BlockSpec((tm, D_MODEL), lambda i, j: (i, 0)), wspec, wspec],
        out_specs=pl.BlockSpec((tm, tn), lambda i, j: (i, j)),
        out_shape=jax.ShapeDtypeStruct((t, D_FF), BF16),
        compiler_params=_cparams(("arbitrary", "arbitrary"), 56),
        name="ffn_up",
    )(h, wg, wu)


def kernel(x_prompt, x_sample, state_s5_re, state_s5_im, state_lru, cache_conv, g_pre_mix, w_in, s5_lam_re, s5_lam_im, s5_log_step, s5_b_re, s5_b_im, s5_c_re, s5_c_im, s5_d, w_glu, b_glu, conv_w, conv_b, lru_w_r, lru_b_r, lru_w_i, lru_b_i, lru_lam, p_a, p_b, w_out, g_post_mix, g_pre_ffn, w_gate, w_up, w_down, g_post_ffn):
    depth = w_in.shape[0]
    nbp, lp, _ = x_prompt.shape
    nbs, ls, _ = x_sample.shape
    sd = state_lru.dtype
    rows_p = nbp * lp
    rows_s = nbs * ls
    hist_p = (CONV_W - 1) * nbp
    hist_s = (CONV_W - 1) * nbs
    rows = rows_p + rows_s
    tm = 1280 if rows % 1280 == 0 else 256
    tm_half = tm // 2
    tl_s5 = min(512, lp)
    tl_lru = min(128, lp)
    seg_p = dict(nb=nbp, seq=lp, row0=0)
    seg_s = dict(nb=nbs, seq=ls, row0=rows_p)

    kb, ct, pp, qq = _s5_prep(s5_lam_re, s5_lam_im, s5_log_step,
                              s5_b_re.transpose(0, 1, 3, 2), s5_b_im.transpose(0, 1, 3, 2),
                              s5_c_re, s5_c_im)
    vec = lambda a: a.reshape(depth, 1, -1)
    w_in_b, w_glu_b, w_out_b, w_gate_b, w_up_b = w_in, w_glu, w_out, w_gate, w_up
    p_a_b = p_a.astype(BF16)
    p_b_b = p_b.astype(BF16)
    w_down_b = w_down.astype(BF16)
    wri = jnp.concatenate([lru_w_r, lru_w_i], axis=-1).astype(BF16)
    g_pre_mix, g_post_mix, g_pre_ffn, g_post_ffn = map(
        vec, (g_pre_mix, g_post_mix, g_pre_ffn, g_post_ffn))
    s5_d, b_glu, conv_b, lru_b_r, lru_b_i, lru_lam = map(
        vec, (s5_d, b_glu, conv_b, lru_b_r, lru_b_i, lru_lam))

    h = _prenorm(x_prompt, g_pre_mix, None, layer=0, row0=0, total_rows=rows)
    h = _prenorm(x_sample, g_pre_mix, h, layer=0, row0=rows_p, total_rows=rows)
    zeros_s5 = jnp.zeros((N_GROUPS, nbp, 2 * P_STATE), F32)
    zeros_lru = jnp.zeros((nbp, D_MODEL), F32)
    zeros_conv = jnp.zeros((hist_p, D_MODEL), F32)
    outs_p, outs_s = [], []
    for l in range(depth):
        z = _matmul(h, w_in_b, layer=l, tm=tm, tn=512)

        ya_f, ya_b, s5p = _s5_branch(z, kb, ct, pp, qq, s5_d, zeros_s5, None,
                                     layer=l, nb=nbp, seq=lp, tl=tl_s5, row0=0)
        s5_h0 =jnp.concatenate([state_s5_re[l], state_s5_im[l]], axis=-1).transpose(1, 0, 2)
        ya_f, ya_b, s5s = _s5_branch(z, kb, ct, pp, qq, s5_d, s5_h0.astype(F32), (ya_f, ya_b),
                                     layer=l, nb=nbs, seq=ls, tl=ls, row0=rows_p)
        lru_args = (conv_w, conv_b, wri, lru_b_r, lru_b_i, lru_lam)
        yb, lrup, convp = _lru_branch(z, *lru_args, zeros_lru, zeros_conv, None,
                                      layer=l, nb=nbp, seq=lp, tl=tl_lru, row0=0)
        cbuf = cache_conv[l].transpose(1, 0, 2).reshape(hist_s, D_MODEL)
        yb, lrus, convs = _lru_branch(z, *lru_args, state_lru[l].astype(F32), cbuf.astype(F32),
                                      yb, layer=l, nb=nbs, seq=ls, tl=ls, row0=rows_p)
        outs_p.append((s5p, lrup, convp))
        outs_s.append((s5s, lrus, convs))

        ya2 = _glu(ya_b, ya_f, w_glu_b, b_glu, layer=l, tm=tm, tn=512)
        merged = _merge(ya2, p_a_b, yb, p_b_b, z, layer=l, tm=tm_half, tn=512)
        f = _matmul(merged, w_out_b, layer=l, tm=tm, tn=512)
        if l == 0:
            x, h = _finish(x_prompt, f, g_post_mix, l, g_pre_ffn, l, None,
                           x_bm=True, out_bm=False, **seg_p)
            x, h = _finish(x_sample, f, g_post_mix, l, g_pre_ffn, l, (x, h),
                           x_bm=True, out_bm=False, **seg_s)
        else:
            x, h = _finish(x, f, g_post_mix, l, g_pre_ffn, l, None,
                           nb=1, seq=rows, row0=0, x_bm=False, out_bm=False)
        hmid = _ffn_up(h, w_gate_b, w_up_b, layer=l, tm=tm, tn=256)
        f = _matmul(hmid, w_down_b, layer=l, tm=tm_half, tn=512, vmem_mib=58)
        if l + 1 < depth:
            x, h = _finish(x, f, g_post_ffn, l, g_pre_mix, l + 1, None,
                           nb=1, seq=rows, row0=0, x_bm=False, out_bm=False)
        else:
            y_prompt, _ = _finish(x, f, g_post_ffn, l, None, None, None,
                                  x_bm=False, out_bm=True, **seg_p)
            y_sample, _ = _finish(x, f, g_post_ffn, l, None, None, None,
                                  x_bm=False, out_bm=True, **seg_s)

    def pack(outs, nb):
        s5 = jnp.stack([o[0] for o in outs]).transpose(0, 2, 1, 3)
        lru = jnp.stack([o[1] for o in outs])
        conv = jnp.stack([o[2] for o in outs]).reshape(depth, CONV_W - 1, nb, D_MODEL)
        return (s5[..., :P_STATE].astype(sd), s5[..., P_STATE:].astype(sd), lru.astype(sd),
                conv.transpose(0, 2, 1, 3).astype(sd))

    p_re, p_im, p_lru, p_conv = pack(outs_p, nbp)
    s_re, s_im, s_lru, s_conv = pack(outs_s, nbs)
    return (y_prompt, y_sample, p_re, p_im, p_lru, p_conv, s_re, s_im, s_lru, s_conv)
```
